```python
import jax, jax.numpy as jnp
from jax import lax
import numpy as np

D_MODEL = 4096
BATCH = 2
SEQ = 8192
DEPTH = 2
DEC_BATCH = 32
DEC_SEQ = 64
PAST_LEN = 2048

CHUNK = 64
HEAD_DIM = 128
H_A = 12
H_B = 8
H_C = 6
DK_C = 128
DV_C = 256
W_A = H_A * HEAD_DIM
W_B = H_B * HEAD_DIM
QK_C = H_C * DK_C
W_C = H_C * DV_C
MIX_WIDTH = W_A + W_B + W_C
GATE_RANK = 16
GLA_TAU = 16.0
Q_BLOCK = 128
BAND_CHUNKS = 8
BAND_PAST = BAND_CHUNKS * CHUNK
MAX_REL = 128
D_FF = 14336
N_EXPERTS = 8
TOP_K = 2
N_DENSE = (DEPTH + 1) // 2
N_MOE = DEPTH // 2
ALPHA = (2.0 * DEPTH) ** 0.25
BETA = (8.0 * DEPTH) ** -0.25
LN_EPS = 1e-5
RMS_EPS = 1e-6
ATTN_SCALE = HEAD_DIM ** -0.5
PROJ_SIZES = (W_A, W_A, W_A, H_A, W_B, W_B, W_B, QK_C, QK_C, W_C, W_C, GATE_RANK)
N_IN = sum(PROJ_SIZES)

kernel_name = "hybrid_fox_band_gla_streaming_step"


def layer_norm(x, g, b):
    xf = x.astype(jnp.float32)
    mu = jnp.mean(xf, axis=-1, keepdims=True)
    var = jnp.mean(jnp.square(xf - mu), axis=-1, keepdims=True)
    y = (xf - mu) * lax.rsqrt(var + LN_EPS) * g.astype(jnp.float32) + b.astype(jnp.float32)
    return y.astype(x.dtype)


def group_inputs(h, w_in, b_forget, w_gate_up, b_gate):
    B, L, _ = h.shape
    z = jnp.einsum('bld,dn->bln', h, w_in)
    split_at = np.cumsum(PROJ_SIZES)[:-1].tolist()
    qa, ka, va, fa, qb, kb, vb, qc, kc, vc, rc, ac = jnp.split(z, split_at, axis=-1)
    heads = lambda t, n: t.reshape(B, L, n, t.shape[-1] // n)
    logf = jax.nn.log_sigmoid(fa.astype(jnp.float32) + b_forget.astype(jnp.float32))
    g = jax.nn.log_sigmoid(jnp.einsum('blr,rn->bln', ac, w_gate_up).astype(jnp.float32)
                           + b_gate.astype(jnp.float32)) / GLA_TAU
    return (heads(qa, H_A), heads(ka, H_A), heads(va, H_A), logf,
            heads(qb, H_B), heads(kb, H_B), heads(vb, H_B),
            heads(qc, H_C) * (DK_C ** -0.5), heads(kc, H_C), heads(vc, H_C), rc, heads(g, H_C))


def fox_attend(q, k, v, cq, ck, q_pos, k_pos):
    s = jnp.einsum('bqhd,bkhd->bhqk', q, k, preferred_element_type=jnp.float32) * ATTN_SCALE
    decay = jnp.transpose(cq, (0, 2, 1))[:, :, :, None] - jnp.transpose(ck, (0, 2, 1))[:, :, None, :]
    mask = k_pos[None, :] <= q_pos[:, None]
    p = jax.nn.softmax(jnp.where(mask, s + decay, -jnp.inf), axis=-1)
    return jnp.einsum('bhqk,bkhd->bqhd', p.astype(v.dtype), v)


def fox_prompt(q, k, v, c):
    B, L, H, Dh = q.shape
    k_pos = jnp.arange(L)

    def block(i):
        start = i * Q_BLOCK
        q_i = lax.dynamic_slice_in_dim(q, start, Q_BLOCK, axis=1)
        c_i = lax.dynamic_slice_in_dim(c, start, Q_BLOCK, axis=1)
        return fox_attend(q_i, k, v, c_i, c, start + jnp.arange(Q_BLOCK), k_pos)

    o = lax.map(block, jnp.arange(L // Q_BLOCK))
    return jnp.moveaxis(o, 0, 1).reshape(B, L, H, Dh)


def band_attend(q, k, v, q_pos, k_pos, rel_bias):
    s = jnp.einsum('bqhd,bkhd->bhqk', q, k, preferred_element_type=jnp.float32) * ATTN_SCALE
    rel = jnp.clip(k_pos[None, :] - q_pos[:, None], -MAX_REL, MAX_REL) + MAX_REL
    bias = jnp.transpose(rel_bias[rel], (2, 0, 1)).astype(jnp.float32)[None]
    q_chunk = q_pos // CHUNK
    k_chunk = k_pos // CHUNK
    mask = ((k_pos[None, :] >= 0) & (k_chunk[None, :] <= q_chunk[:, None])
            & (k_chunk[None, :] >= q_chunk[:, None] - BAND_CHUNKS))
    p = jax.nn.softmax(jnp.where(mask, s + bias, -jnp.inf), axis=-1)
    return jnp.einsum('bhqk,bkhd->bqhd', p.astype(v.dtype), v)


def band_prompt(q, k, v, rel_bias):
    B, L, H, Dh = q.shape
    pad = ((0, 0), (BAND_PAST, 0), (0, 0), (0, 0))
    kp = jnp.pad(k, pad)
    vp = jnp.pad(v, pad)
    band = BAND_PAST + CHUNK

    def chunk(ci):
        start = ci * CHUNK
        q_i = lax.dynamic_slice_in_dim(q, start, CHUNK, axis=1)
        k_i = lax.dynamic_slice_in_dim(kp, start, band, axis=1)
        v_i = lax.dynamic_slice_in_dim(vp, start, band, axis=1)
        q_pos = start + jnp.arange(CHUNK)
        k_pos = start - BAND_PAST + jnp.arange(band)
        return band_attend(q_i, k_i, v_i, q_pos, k_pos, rel_bias)

    o = lax.map(chunk, jnp.arange(L // CHUNK))
    return jnp.moveaxis(o, 0, 1).reshape(B, L, H, Dh)


def gla_chunk(S, inputs):
    q, k, v, g = inputs
    C = q.shape[1]
    G = jnp.cumsum(g, axis=1)
    o_inter = jnp.einsum('bthk,bhkv->bthv', q * jnp.exp(G), S)
    causal = jnp.tril(jnp.ones((C, C), dtype=bool))[None, :, :, None, None]
    diff = G[:, :, None] - G[:, None, :]
    decay = jnp.exp(jnp.where(causal, diff, -jnp.inf))
    A = jnp.einsum('bthk,btshk->bhts', q, decay * k[:, None])
    o_intra = jnp.einsum('bhts,bshv->bthv', A, v)
    G_last = G[:, -1]
    k_dec = k * jnp.exp(G_last[:, None] - G)
    S_new = jnp.exp(G_last)[..., None] * S + jnp.einsum('bshk,bshv->bhkv', k_dec, v)
    return S_new, o_inter + o_intra


def gla_prompt(q, k, v, g):
    B, L = q.shape[:2]
    nc = L // CHUNK
    to_chunks = lambda t: jnp.moveaxis(t.reshape((B, nc, CHUNK) + t.shape[2:]), 1, 0)
    S0 = jnp.zeros((B, H_C, DK_C, DV_C), jnp.float32)
    S, o = lax.scan(gla_chunk, S0, (to_chunks(q), to_chunks(k), to_chunks(v), to_chunks(g)))
    return S, jnp.moveaxis(o, 0, 1).reshape(B, L, H_C, DV_C)


def merge_out(o_a, o_b, o_c, rc, gla_norm_g, w_out):
    B, L = o_a.shape[:2]
    oc = o_c.astype(jnp.float32)
    oc = oc * lax.rsqrt(jnp.mean(jnp.square(oc), axis=-1, keepdims=True) + RMS_EPS) * gla_norm_g.astype(jnp.float32)
    oc = oc.reshape(B, L, W_C) * jax.nn.silu(rc.astype(jnp.float32))
    o = jnp.concatenate([o_a.reshape(B, L, W_A), o_b.reshape(B, L, W_B), oc.astype(o_a.dtype)], axis=-1)
    return jnp.einsum('bln,nd->bld', o, w_out)


def mixer_prompt(h, w_in, b_forget, rel_bias, w_gate_up, b_gate, gla_norm_g, w_out):
    qa, ka, va, logf, qb, kb, vb, qc, kc, vc, rc, gc = group_inputs(h, w_in, b_forget, w_gate_up, b_gate)
    L = h.shape[1]
    c = jnp.cumsum(logf, axis=1)
    o_a = fox_prompt(qa, ka, va, c)
    o_b = band_prompt(qb, kb, vb, rel_bias)
    S, o_c = gla_prompt(qc, kc, vc, gc)
    y = merge_out(o_a, o_b, o_c, rc, gla_norm_g, w_out)
    n_band = min(BAND_PAST, L)
    return y, (ka, va, logf, kb[:, L - n_band:], vb[:, L - n_band:], S)


def mixer_sample(h, c_fk, c_fv, c_flogf, c_bk, c_bv, s_gla,
                 w_in, b_forget, rel_bias, w_gate_up, b_gate, gla_norm_g, w_out):
    qa, ka, va, logf, qb, kb, vb, qc, kc, vc, rc, gc = group_inputs(h, w_in, b_forget, w_gate_up, b_gate)
    T = h.shape[1]
    P = c_fk.shape[1]
    q_pos = P + jnp.arange(T)
    c_all = jnp.cumsum(jnp.concatenate([c_flogf.astype(jnp.float32), logf], axis=1), axis=1)
    k_all = jnp.concatenate([c_fk.astype(ka.dtype), ka], axis=1)
    v_all = jnp.concatenate([c_fv.astype(va.dtype), va], axis=1)
    o_a = fox_attend(qa, k_all, v_all, c_all[:, P:], c_all, q_pos, jnp.arange(P + T))
    nb = c_bk.shape[1]
    kb_all = jnp.concatenate([c_bk.astype(kb.dtype), kb], axis=1)
    vb_all = jnp.concatenate([c_bv.astype(vb.dtype), vb], axis=1)
    o_b = band_attend(qb, kb_all, vb_all, q_pos, P - nb + jnp.arange(nb + T), rel_bias)
    S_new, o_c = gla_chunk(s_gla.astype(jnp.float32), (qc, kc, vc, gc))
    y = merge_out(o_a, o_b, o_c, rc, gla_norm_g, w_out)
    return y, (ka, va, logf, kb_all[:, T:], vb_all[:, T:], S_new)


def swiglu(h, w1, w3, w2):
    a = jax.nn.silu(jnp.einsum('bld,df->blf', h, w1)) * jnp.einsum('bld,df->blf', h, w3)
    return jnp.einsum('blf,fd->bld', a, w2)


def moe_swiglu(h, w_router, w1, w3, w2):
    logits = jnp.einsum('bld,de->ble', h, w_router).astype(jnp.float32)
    top_vals, top_idx = lax.top_k(logits, TOP_K)
    gates = jax.nn.softmax(top_vals, axis=-1)
    combine = jnp.sum(jax.nn.one_hot(top_idx, N_EXPERTS, dtype=jnp.float32) * gates[..., None], axis=-2)
    out = jnp.zeros(h.shape, jnp.float32)
    for e in range(N_EXPERTS):
        out = out + combine[..., e:e + 1] * swiglu(h, w1[e], w3[e], w2[e]).astype(jnp.float32)
    return out.astype(h.dtype)


def setup_inputs(seed: int = 0) -> dict:
    key = jax.random.key(seed)
    ks = jax.random.split(key, 32)
    nrm = lambda k, shape, scale: jax.random.normal(k, shape, jnp.float32) * scale
    b_rows = min(BAND_PAST, PAST_LEN)
    col_scale = np.concatenate([
        np.ones(2 * W_A), np.full(W_A, BETA), np.ones(H_A),
        np.ones(2 * W_B), np.full(W_B, BETA),
        np.ones(2 * QK_C), np.full(W_C, BETA), np.ones(W_C + GATE_RANK)]).astype(np.float32)
    inp = {}
    inp['x_prompt'] = nrm(ks[0], (BATCH, SEQ, D_MODEL), 1.0)
    inp['x_sample'] = nrm(ks[1], (DEC_BATCH, DEC_SEQ, D_MODEL), 1.0)
    inp['cache_fox_k'] = nrm(ks[2], (DEPTH, DEC_BATCH, PAST_LEN, H_A, HEAD_DIM), 1.0)
    inp['cache_fox_v'] = nrm(ks[3], (DEPTH, DEC_BATCH, PAST_LEN, H_A, HEAD_DIM), BETA)
    inp['cache_fox_logf'] = jax.nn.log_sigmoid(3.0 + nrm(ks[4], (DEPTH, DEC_BATCH, PAST_LEN, H_A), 1.0))
    inp['cache_band_k'] = nrm(ks[5], (DEPTH, DEC_BATCH, b_rows, H_B, HEAD_DIM), 1.0)
    inp['cache_band_v'] = nrm(ks[6], (DEPTH, DEC_BATCH, b_rows, H_B, HEAD_DIM), BETA)
    inp['state_gla'] = nrm(ks[7], (DEPTH, DEC_BATCH, H_C, DK_C, DV_C), 0.5)
    inp['w_in'] = nrm(ks[8], (DEPTH, D_MODEL, N_IN), D_MODEL ** -0.5) * jnp.asarray(col_scale)
    inp['b_forget'] = 3.0 + nrm(ks[9], (DEPTH, H_A), 0.1)
    inp['rel_bias'] = nrm(ks[10], (DEPTH, 2 * MAX_REL + 1, H_B), 0.1)
    inp['w_gate_up'] = nrm(ks[11], (DEPTH, GATE_RANK, QK_C), GATE_RANK ** -0.5)
    inp['b_gate'] = nrm(ks[12], (DEPTH, QK_C), 0.1)
    inp['gla_norm_g'] = 1.0 + nrm(ks[13], (DEPTH, DV_C), 0.01)
    inp['w_out'] = nrm(ks[14], (DEPTH, MIX_WIDTH, D_MODEL), BETA * MIX_WIDTH ** -0.5)
    inp['ln1_g'] = 1.0 + nrm(ks[15], (DEPTH, D_MODEL), 0.01)
    inp['ln1_b'] = nrm(ks[16], (DEPTH, D_MODEL), 0.01)
    inp['ln2_g'] = 1.0 + nrm(ks[17], (DEPTH, D_MODEL), 0.01)
    inp['ln2_b'] = nrm(ks[18], (DEPTH, D_MODEL), 0.01)
    inp['ffn_w1'] = nrm(ks[19], (N_DENSE, D_MODEL, D_FF), D_MODEL ** -0.5)
    inp['ffn_w3'] = nrm(ks[20], (N_DENSE, D_MODEL, D_FF), D_MODEL ** -0.5)
    inp['ffn_w2'] = nrm(ks[21], (N_DENSE, D_FF, D_MODEL), BETA * D_FF ** -0.5)
    inp['moe_router'] = nrm(ks[22], (N_MOE, D_MODEL, N_EXPERTS), D_MODEL ** -0.5)
    inp['moe_w1'] = nrm(ks[23], (N_MOE, N_EXPERTS, D_MODEL, D_FF), D_MODEL ** -0.5)
    inp['moe_w3'] = nrm(ks[24], (N_MOE, N_EXPERTS, D_MODEL, D_FF), D_MODEL ** -0.5)
    inp['moe_w2'] = nrm(ks[25], (N_MOE, N_EXPERTS, D_FF, D_MODEL), BETA * D_FF ** -0.5)
    return inp


def reference(x_prompt, x_sample, cache_fox_k, cache_fox_v, cache_fox_logf, cache_band_k, cache_band_v, state_gla,
              w_in, b_forget, rel_bias, w_gate_up, b_gate, gla_norm_g, w_out,
              ln1_g, ln1_b, ln2_g, ln2_b, ffn_w1, ffn_w3, ffn_w2,
              moe_router, moe_w1, moe_w3, moe_w2):
    yp, ys = x_prompt, x_sample
    sp_all, ss_all = [], []
    for i in range(DEPTH):
        mp, st_p = mixer_prompt(yp, w_in[i], b_forget[i], rel_bias[i], w_gate_up[i], b_gate[i], gla_norm_g[i], w_out[i])
        ms, st_s = mixer_sample(ys, cache_fox_k[i], cache_fox_v[i], cache_fox_logf[i], cache_band_k[i], cache_band_v[i],
                                state_gla[i], w_in[i], b_forget[i], rel_bias[i], w_gate_up[i], b_gate[i],
                                gla_norm_g[i], w_out[i])
        sp_all.append(st_p)
        ss_all.append(st_s)
        yp = layer_norm(ALPHA * yp + mp, ln1_g[i], ln1_b[i])
        ys = layer_norm(ALPHA * ys + ms, ln1_g[i], ln1_b[i])
        j = i // 2
        if i % 2 == 0:
            fp = swiglu(yp, ffn_w1[j], ffn_w3[j], ffn_w2[j])
            fs = swiglu(ys, ffn_w1[j], ffn_w3[j], ffn_w2[j])
        else:
            fp = moe_swiglu(yp, moe_router[j], moe_w1[j], moe_w3[j], moe_w2[j])
            fs = moe_swiglu(ys, moe_router[j], moe_w1[j], moe_w3[j], moe_w2[j])
        yp = layer_norm(ALPHA * yp + fp, ln2_g[i], ln2_b[i])
        ys = layer_norm(ALPHA * ys + fs, ln2_g[i], ln2_b[i])
    stack = lambda states, n: jnp.stack([s[n] for s in states], axis=0)
    new_fox_k_prompt = stack(sp_all, 0)
    new_fox_v_prompt = stack(sp_all, 1)
    new_fox_logf_prompt = stack(sp_all, 2)
    new_band_k_prompt = stack(sp_all, 3)
    new_band_v_prompt = stack(sp_all, 4)
    new_gla_prompt = stack(sp_all, 5)
    new_fox_k_sample = stack(ss_all, 0)
    new_fox_v_sample = stack(ss_all, 1)
    new_fox_logf_sample = stack(ss_all, 2)
    new_band_k_sample = stack(ss_all, 3)
    new_band_v_sample = stack(ss_all, 4)
    new_gla_sample = stack(ss_all, 5)
    return (yp, ys,
            new_fox_k_prompt, new_fox_v_prompt, new_fox_logf_prompt, new_band_k_prompt, new_band_v_prompt, new_gla_prompt,
            new_fox_k_sample, new_fox_v_sample, new_fox_logf_sample, new_band_k_sample, new_band_v_sample, new_gla_sample)
```

```python
import functools
import math

import jax
import jax.numpy as jnp
import numpy as np
from jax import lax
from jax.experimental import pallas as pl
from jax.experimental.pallas import tpu as pltpu

CHUNK = 64
HEAD_DIM = 128
H_A = 12
H_B = 8
H_C = 6
DK_C = 128
DV_C = 256
W_A = H_A * HEAD_DIM
W_B = H_B * HEAD_DIM
QK_C = H_C * DK_C
W_C = H_C * DV_C
GATE_RANK = 16
GLA_TAU = 16.0
BAND_CHUNKS = 8
BAND_PAST = BAND_CHUNKS * CHUNK
MAX_REL = 128
TOP_K = 2
LN_EPS = 1e-5
RMS_EPS = 1e-6
ATTN_SCALE = HEAD_DIM ** -0.5

LANE = 128
NEG_INF = float("-inf")
F32 = jnp.float32
BF16 = jnp.bfloat16
VMEM_LIMIT = 56 * 1024 * 1024

ST_KA, ST_VA, ST_KB, ST_VB = 0, W_A, 2 * W_A, 2 * W_A + W_B
W_STATE = 2 * W_A + 2 * W_B
RS_QC, RS_KC, RS_VC, RS_RC, RS_QA, RS_QB = 0, QK_C, 2 * QK_C, 2 * QK_C + W_C, 2 * QK_C + 2 * W_C, 2 * QK_C + 2 * W_C + W_A
W_REST = RS_QB + W_B


def _tile(dim, pref, mult):
    if dim <= pref:
        return dim
    t = (pref // mult) * mult
    while t > mult and dim % t:
        t -= mult
    assert dim % t == 0, (dim, pref, mult)
    return t


def _cparams(sem):
    return pltpu.CompilerParams(dimension_semantics=sem, vmem_limit_bytes=VMEM_LIMIT)


def _log_sigmoid(x):
    return jnp.minimum(x, 0.0) - jnp.log(1.0 + jnp.exp(-jnp.abs(x)))


def _silu(x):
    return x / (1.0 + jnp.exp(-x))


def _layer_norm(x, g, b):
    mu = jnp.mean(x, axis=-1, keepdims=True)
    xc = x - mu
    var = jnp.mean(xc * xc, axis=-1, keepdims=True)
    return xc * lax.rsqrt(var + LN_EPS) * g + b


def _wres_kernel(te_ref, nu_ref, x_ref, *refs, n_w, epilogue):
    w_refs = refs[:n_w]
    out_refs = refs[n_w:]
    t = pl.program_id(1)

    @pl.when(t < nu_ref[0])
    def _():
        x = x_ref[...]
        accs = [jnp.dot(x, w[0], preferred_element_type=F32) for w in w_refs]
        epilogue(accs, out_refs)

    @pl.when(t >= nu_ref[0])
    def _():
        for o in out_refs:
            o[...] = jnp.zeros(o.shape, o.dtype)


def _matmul_wres(x, ws, tile_expert, n_used, *, tm, tn, col0, n_cols, out_dtypes, epilogue, name):
    rows, kdim = x.shape
    nt = rows // tm
    nj = n_cols // tn
    j0 = col0 // tn
    assert rows % tm == 0 and n_cols % tn == 0 and col0 % tn == 0
    grid_spec = pltpu.PrefetchScalarGridSpec(
        num_scalar_prefetch=2,
        grid=(nj, nt),
        in_specs=[pl.BlockSpec((tm, kdim), lambda j, t, te, nu: (t, 0))]
        + [pl.BlockSpec((1, kdim, tn), lambda j, t, te, nu: (te[t], 0, j + j0)) for _ in ws],
        out_specs=[pl.BlockSpec((tm, tn), lambda j, t, te, nu: (t, j)) for _ in out_dtypes],
    )
    return pl.pallas_call(
        functools.partial(_wres_kernel, n_w=len(ws), epilogue=epilogue),
        grid_spec=grid_spec,
        out_shape=[jax.ShapeDtypeStruct((rows, n_cols), dt) for dt in out_dtypes],
        compiler_params=_cparams(("arbitrary", "arbitrary")),
        name=name,
    )(tile_expert, n_used, x, *ws)


def _epi_store(accs, outs):
    for o in outs:
        o[...] = accs[0].astype(o.dtype)


def _epi_swiglu(accs, outs):
    outs[0][...] = (_silu(accs[0]) * accs[1]).astype(outs[0].dtype)


def _acc_kernel(te_ref, nu_ref, a_ref, w_ref, *refs, n_extra, epilogue):
    extra = refs[:n_extra]
    outs = refs[n_extra:-1]
    acc_ref = refs[-1]
    t = pl.program_id(0)
    k = pl.program_id(1)
    nk = pl.num_programs(1)
    used = t < nu_ref[0]

    @pl.when(used & (k == 0))
    def _():
        acc_ref[...] = jnp.dot(a_ref[...], w_ref[0], preferred_element_type=F32)

    @pl.when(used & (k > 0))
    def _():
        acc_ref[...] += jnp.dot(a_ref[...], w_ref[0], preferred_element_type=F32)

    @pl.when(used & (k == nk - 1))
    def _():
        epilogue(acc_ref[...], extra, outs)

    @pl.when(jnp.logical_not(used) & (k == nk - 1))
    def _():
        for o in outs:
            o[...] = jnp.zeros(o.shape, o.dtype)


def _matmul_acc(a, w, tile_expert, n_used, extra, extra_specs, *, tm, tk, out_dtypes, epilogue, name):
    rows, kdim = a.shape
    n_out = w.shape[2]
    nt = rows // tm
    nk = kdim // tk
    assert rows % tm == 0 and kdim % tk == 0
    grid_spec = pltpu.PrefetchScalarGridSpec(
        num_scalar_prefetch=2,
        grid=(nt, nk),
        in_specs=[pl.BlockSpec((tm, tk), lambda t, k, te, nu: (t, k)),
                  pl.BlockSpec((1, tk, n_out), lambda t, k, te, nu: (te[t], k, 0))] + extra_specs,
        out_specs=[pl.BlockSpec((tm, n_out), lambda t, k, te, nu: (t, 0)) for _ in out_dtypes],
        scratch_shapes=[pltpu.VMEM((tm, n_out), F32)],
    )
    return pl.pallas_call(
        functools.partial(_acc_kernel, n_extra=len(extra), epilogue=epilogue),
        grid_spec=grid_spec,
        out_shape=[jax.ShapeDtypeStruct((rows, n_out), dt) for dt in out_dtypes],
        compiler_params=_cparams(("arbitrary", "arbitrary")),
        name=name,
    )(tile_expert, n_used, a, w, *extra)


def _epi_resid_ln(acc, extra, outs, *, alpha):
    x_ref, g_ref, b_ref = extra
    y = _layer_norm(alpha * x_ref[...] + acc, g_ref[...], b_ref[...])
    outs[0][...] = y
    outs[1][...] = y.astype(BF16)


def _epi_row_scale(acc, extra, outs):
    outs[0][...] = acc * extra[0][...]


def _proj_resid_ln(a, w, x_f32, ln_g, ln_b, alpha, *, tm, tk, name):
    rows = a.shape[0]
    d = w.shape[-1]
    te = jnp.zeros((rows // tm,), jnp.int32)
    nu = jnp.full((1,), rows // tm, jnp.int32)
    extra_specs = [pl.BlockSpec((tm, d), lambda t, k, te, nu: (t, 0)),
                   pl.BlockSpec((1, d), lambda t, k, te, nu: (0, 0)),
                   pl.BlockSpec((1, d), lambda t, k, te, nu: (0, 0))]
    return _matmul_acc(a, w.reshape((1,) + w.shape[-2:]), te, nu,
                       [x_f32, ln_g.reshape(1, d), ln_b.reshape(1, d)], extra_specs,
                       tm=tm, tk=tk, out_dtypes=[F32, BF16],
                       epilogue=functools.partial(_epi_resid_ln, alpha=alpha), name=name)


def _forget_kernel(wt_ref, b_ref, x_ref, o_ref):
    fa = lax.dot_general(wt_ref[...], x_ref[...], (((1,), (1,)), ((), ())), preferred_element_type=F32)
    o_ref[...] = _log_sigmoid(fa + b_ref[...])


def _forget_logits(x_bf, w_fa_t, b_col, *, tm):
    n, d = x_bf.shape
    hp = w_fa_t.shape[0]
    return pl.pallas_call(
        _forget_kernel,
        grid=(n // tm,),
        in_specs=[pl.BlockSpec((hp, d), lambda t: (0, 0)),
                  pl.BlockSpec((hp, 1), lambda t: (0, 0)),
                  pl.BlockSpec((tm, d), lambda t: (t, 0))],
        out_specs=pl.BlockSpec((hp, tm), lambda t: (0, t)),
        out_shape=jax.ShapeDtypeStruct((hp, n), F32),
        compiler_params=_cparams(("arbitrary",)),
        name="forget_logits",
    )(w_fa_t, b_col, x_bf)


def _gate_kernel(x_ref, wa_ref, wu_ref, b_ref, o_ref):
    ac = jnp.dot(x_ref[...], wa_ref[...], preferred_element_type=F32)
    pre = jnp.dot(ac.astype(BF16), wu_ref[...], preferred_element_type=F32) + b_ref[...]
    o_ref[...] = _log_sigmoid(pre) * (1.0 / GLA_TAU)


def _gla_gates(x_bf, w_ac_pad, w_up_pad, b_gate, *, tm):
    n, d = x_bf.shape
    return pl.pallas_call(
        _gate_kernel,
        grid=(n // tm,),
        in_specs=[pl.BlockSpec((tm, d), lambda t: (t, 0)),
                  pl.BlockSpec((d, LANE), lambda t: (0, 0)),
                  pl.BlockSpec((LANE, QK_C), lambda t: (0, 0)),
                  pl.BlockSpec((1, QK_C), lambda t: (0, 0))],
        out_specs=pl.BlockSpec((tm, QK_C), lambda t: (t, 0)),
        out_shape=jax.ShapeDtypeStruct((n, QK_C), F32),
        compiler_params=_cparams(("arbitrary",)),
        name="gla_gates",
    )(x_bf, w_ac_pad, w_up_pad, b_gate.reshape(1, QK_C))


def _cumsum_kernel(x_ref, o_ref, *, t_len):
    rows = x_ref.shape[0]
    ii = lax.broadcasted_iota(jnp.int32, (LANE, LANE), 0)
    jj = lax.broadcasted_iota(jnp.int32, (LANE, LANE), 1)
    upper = (ii <= jj).astype(F32)
    carry = jnp.zeros((rows, 1), F32)
    for c0 in range(0, t_len, LANE):
        w = min(LANE, t_len - c0)
        blk = x_ref[:, c0:c0 + w]
        cs = jnp.dot(blk, upper[:w, :w], preferred_element_type=F32, precision=lax.Precision.HIGHEST) + carry
        o_ref[:, c0:c0 + w] = cs
        carry = cs[:, w - 1:w]


def _cumsum_last(x):
    rows, t_len = x.shape
    rb = 8
    assert rows % rb == 0
    return pl.pallas_call(
        functools.partial(_cumsum_kernel, t_len=t_len),
        grid=(rows // rb,),
        in_specs=[pl.BlockSpec((rb, t_len), lambda r: (r, 0))],
        out_specs=pl.BlockSpec((rb, t_len), lambda r: (r, 0)),
        out_shape=jax.ShapeDtypeStruct((rows, t_len), F32),
        compiler_params=_cparams(("arbitrary",)),
        name="cumsum_time",
    )(x)


def _fox_prompt_kernel(q_ref, k_ref, v_ref, c_ref, o_ref, *, tq):
    qi = pl.program_id(2)
    q = q_ref[...]
    q0 = pl.multiple_of(qi * tq, tq)
    c_first = c_ref[0, :, pl.ds(q0, LANE)][:, :1]

    def step(k0, carry, masked):
        m, l, acc = carry
        kb = k_ref[pl.ds(k0, tq), :]
        vb = v_ref[pl.ds(k0, tq), :]
        s = lax.dot_general(q, kb, (((1,), (1,)), ((), ())), preferred_element_type=F32) * ATTN_SCALE
        s = s + (c_first - c_ref[0, :, pl.ds(k0, tq)])
        if masked:
            ii = lax.broadcasted_iota(jnp.int32, (tq, tq), 0)
            jj = lax.broadcasted_iota(jnp.int32, (tq, tq), 1)
            s = jnp.where(jj <= ii, s, NEG_INF)
        m_new = jnp.maximum(m, jnp.max(s, axis=1, keepdims=True))
        a = jnp.exp(m - m_new)
        p = jnp.exp(s - m_new)
        l = a * l + jnp.sum(p, axis=1, keepdims=True)
        acc = a * acc + jnp.dot(p.astype(BF16), vb, preferred_element_type=F32)
        return m_new, l, acc

    init = (jnp.full((tq, 1), NEG_INF, F32), jnp.zeros((tq, 1), F32), jnp.zeros((tq, HEAD_DIM), F32))
    carry = lax.fori_loop(0, qi, lambda kj, c: step(pl.multiple_of(kj * tq, tq), c, False), init)
    m, l, acc = step(q0, carry, True)
    o_ref[...] = (acc / l).astype(o_ref.dtype)


def _fox_prompt(zq, zkv, c_rows, batch, seq, *, tq):
    nq = seq // tq
    q_col = RS_QA // HEAD_DIM
    k_col = ST_KA // HEAD_DIM
    v_col = ST_VA // HEAD_DIM
    return pl.pallas_call(
        functools.partial(_fox_prompt_kernel, tq=tq),
        grid=(batch, H_A, nq),
        in_specs=[pl.BlockSpec((tq, HEAD_DIM), lambda b, h, i: (b * nq + i, q_col + h)),
                  pl.BlockSpec((seq, HEAD_DIM), lambda b, h, i: (b, k_col + h)),
                  pl.BlockSpec((seq, HEAD_DIM), lambda b, h, i: (b, v_col + h)),
                  pl.BlockSpec((1, 1, seq), lambda b, h, i: (h * batch + b, 0, 0))],
        out_specs=pl.BlockSpec((tq, HEAD_DIM), lambda b, h, i: (b * nq + i, h)),
        out_shape=jax.ShapeDtypeStruct((batch * seq, W_A), BF16),
        compiler_params=_cparams(("arbitrary", "arbitrary", "arbitrary")),
        name="fox_prompt",
    )(zq, zkv, zkv, c_rows)


def _fox_sample_kernel(q_ref, kn_ref, vn_ref, ck_ref, cv_ref, c_ref, o_ref, *, hg, past, t_new):
    ii = lax.broadcasted_iota(jnp.int32, (t_new, t_new), 0)
    jj = lax.broadcasted_iota(jnp.int32, (t_new, t_new), 1)
    causal = jj <= ii
    for hh in range(hg):
        sl = slice(hh * HEAD_DIM, (hh + 1) * HEAD_DIM)
        q = q_ref[:, sl]
        kc = ck_ref[0, :, sl].astype(BF16)
        vc = cv_ref[0, :, sl].astype(BF16)
        kn = kn_ref[:, sl]
        vn = vn_ref[:, sl]
        c_row = c_ref[0, hh:hh + 1, :]
        c_first = c_row[:, past:past + 1]
        s_c = lax.dot_general(q, kc, (((1,), (1,)), ((), ())), preferred_element_type=F32) * ATTN_SCALE
        s_c = s_c + (c_first - c_row[:, :past])
        s_n = lax.dot_general(q, kn, (((1,), (1,)), ((), ())), preferred_element_type=F32) * ATTN_SCALE
        s_n = jnp.where(causal, s_n + (c_first - c_row[:, past:]), NEG_INF)
        m = jnp.maximum(jnp.max(s_c, axis=1, keepdims=True), jnp.max(s_n, axis=1, keepdims=True))
        p_c = jnp.exp(s_c - m)
        p_n = jnp.exp(s_n - m)
        l = jnp.sum(p_c, axis=1, keepdims=True) + jnp.sum(p_n, axis=1, keepdims=True)
        o = jnp.dot(p_c.astype(BF16), vc, preferred_element_type=F32)
        o = o + jnp.dot(p_n.astype(BF16), vn, preferred_element_type=F32)
        o_ref[:, sl] = (o / l).astype(o_ref.dtype)


def _fox_sample(zq, zkv, cache_k, cache_v, c_all, row_blk0, dec_batch, t_new, *, hg):
    past = cache_k.shape[1]
    ng = H_A // hg
    wg = hg * HEAD_DIM
    q_col = RS_QA // wg
    k_col = ST_KA // wg
    v_col = ST_VA // wg
    return pl.pallas_call(
        functools.partial(_fox_sample_kernel, hg=hg, past=past, t_new=t_new),
        grid=(dec_batch, ng),
        in_specs=[pl.BlockSpec((t_new, wg), lambda b, g: (row_blk0 + b, q_col + g)),
                  pl.BlockSpec((t_new, wg), lambda b, g: (row_blk0 + b, k_col + g)),
                  pl.BlockSpec((t_new, wg), lambda b, g: (row_blk0 + b, v_col + g)),
                  pl.BlockSpec((1, past, wg), lambda b, g: (b, 0, g)),
                  pl.BlockSpec((1, past, wg), lambda b, g: (b, 0, g)),
                  pl.BlockSpec((1, hg, past + t_new), lambda b, g: (b * ng + g, 0, 0))],
        out_specs=pl.BlockSpec((t_new, wg), lambda b, g: (b, g)),
        out_shape=jax.ShapeDtypeStruct((dec_batch * t_new, W_A), BF16),
        compiler_params=_cparams(("arbitrary", "arbitrary")),
        name="fox_sample",
    )(zq, zkv, zkv, cache_k, cache_v, c_all)


def _band_bias_kernel(rb_ref, op_ref, os_ref, *, tq, t_new, n_past):
    h = pl.program_id(0)
    ii = lax.broadcasted_iota(jnp.int32, (LANE, LANE), 0)
    jj = lax.broadcasted_iota(jnp.int32, (LANE, LANE), 1)
    idx_same = jj - ii + MAX_REL
    idx_prev = jnp.maximum(jj - ii - LANE, -MAX_REL) + MAX_REL

    def body(r, carry):
        t_same, t_prev = carry
        val = rb_ref[r * H_B + h]
        return jnp.where(idx_same == r, val, t_same), jnp.where(idx_prev == r, val, t_prev)

    zeros = jnp.zeros((LANE, LANE), F32)
    t_same, t_prev = lax.fori_loop(0, 2 * MAX_REL + 1, body, (zeros, zeros))
    far = jnp.full((LANE, LANE), rb_ref[h], F32)
    neg = jnp.full((LANE, LANE), NEG_INF, F32)

    def block(rel_blk):
        return t_same if rel_blk == 0 else t_prev if rel_blk == -1 else far

    for bi in range(tq // LANE):
        for bj in range(2 * tq // LANE):
            koff = bj * LANE - tq
            rel_blk = koff // LANE - bi
            qc = (bi * LANE + ii) // CHUNK
            kc = (koff + jj + tq) // CHUNK - tq // CHUNK
            vis = (kc <= qc) & (kc >= qc - BAND_CHUNKS)
            tile = neg if rel_blk > 0 else jnp.where(vis, block(rel_blk), NEG_INF)
            op_ref[0, bi * LANE:(bi + 1) * LANE, bj * LANE:(bj + 1) * LANE] = tile

    for j0 in range(0, n_past + t_new, LANE):
        w = min(LANE, n_past + t_new - j0)
        rel_blk = (j0 - n_past) // LANE
        qc = ii // CHUNK
        kc = (j0 + jj) // CHUNK - n_past // CHUNK
        vis = (kc <= qc) & (kc >= qc - BAND_CHUNKS)
        tile = jnp.where(vis, block(rel_blk), NEG_INF)
        os_ref[0, :, j0:j0 + w] = tile[:t_new, :w]


def _band_bias(rel_bias, *, tq, t_new, n_past):
    assert tq % LANE == 0 and n_past % LANE == 0 and t_new <= LANE and LANE % CHUNK == 0
    assert MAX_REL == LANE
    return pl.pallas_call(
        functools.partial(_band_bias_kernel, tq=tq, t_new=t_new, n_past=n_past),
        grid=(H_B,),
        in_specs=[pl.BlockSpec(memory_space=pltpu.SMEM)],
        out_specs=[pl.BlockSpec((1, tq, 2 * tq), lambda h: (h, 0, 0)),
                   pl.BlockSpec((1, t_new, n_past + t_new), lambda h: (h, 0, 0))],
        out_shape=[jax.ShapeDtypeStruct((H_B, tq, 2 * tq), F32),
                   jax.ShapeDtypeStruct((H_B, t_new, n_past + t_new), F32)],
        compiler_params=_cparams(("arbitrary",)),
        name="band_bias",
    )(rel_bias.reshape(-1))


def _band_prompt_kernel(q_ref, k_ref, v_ref, bias_ref, o_ref, *, tq):
    t = pl.program_id(2)
    q = q_ref[...]
    lo = pl.multiple_of(jnp.maximum(t - 1, 0) * tq, tq)
    hi = pl.multiple_of(t * tq, tq)
    nt = (((1,), (1,)), ((), ()))
    s_l = lax.dot_general(q, k_ref[pl.ds(lo, tq), :], nt, preferred_element_type=F32) * ATTN_SCALE
    s_l = jnp.where(t > 0, s_l + bias_ref[0, :, :tq], NEG_INF)
    s_r = lax.dot_general(q, k_ref[pl.ds(hi, tq), :], nt, preferred_element_type=F32) * ATTN_SCALE
    s_r = s_r + bias_ref[0, :, tq:]
    m = jnp.maximum(jnp.max(s_l, axis=1, keepdims=True), jnp.max(s_r, axis=1, keepdims=True))
    p_l = jnp.exp(s_l - m)
    p_r = jnp.exp(s_r - m)
    l = jnp.sum(p_l, axis=1, keepdims=True) + jnp.sum(p_r, axis=1, keepdims=True)
    o = jnp.dot(p_l.astype(BF16), v_ref[pl.ds(lo, tq), :], preferred_element_type=F32)
    o = o + jnp.dot(p_r.astype(BF16), v_ref[pl.ds(hi, tq), :], preferred_element_type=F32)
    o_ref[...] = (o / l).astype(o_ref.dtype)


def _band_prompt(zq, zkv, bias_p, batch, seq, *, tq):
    nq = seq // tq
    q_col = RS_QB // HEAD_DIM
    k_col = ST_KB // HEAD_DIM
    v_col = ST_VB // HEAD_DIM
    return pl.pallas_call(
        functools.partial(_band_prompt_kernel, tq=tq),
        grid=(H_B, batch, nq),
        in_specs=[pl.BlockSpec((tq, HEAD_DIM), lambda h, b, i: (b * nq + i, q_col + h)),
                  pl.BlockSpec((seq, HEAD_DIM), lambda h, b, i: (b, k_col + h)),
                  pl.BlockSpec((seq, HEAD_DIM), lambda h, b, i: (b, v_col + h)),
                  pl.BlockSpec((1, tq, 2 * tq), lambda h, b, i: (h, 0, 0))],
        out_specs=pl.BlockSpec((tq, HEAD_DIM), lambda h, b, i: (b * nq + i, h)),
        out_shape=jax.ShapeDtypeStruct((batch * seq, W_B), BF16),
        compiler_params=_cparams(("arbitrary", "arbitrary", "arbitrary")),
        name="band_prompt",
    )(zq, zkv, zkv, bias_p)


def _band_sample_kernel(q_ref, kn_ref, vn_ref, ck_ref, cv_ref, bias_ref, o_ref, *, n_past):
    nt = (((1,), (1,)), ((), ()))
    for h in range(H_B):
        sl = slice(h * HEAD_DIM, (h + 1) * HEAD_DIM)
        q = q_ref[:, sl]
        kc = ck_ref[0, :, sl].astype(BF16)
        vc = cv_ref[0, :, sl].astype(BF16)
        s_c = lax.dot_general(q, kc, nt, preferred_element_type=F32) * ATTN_SCALE + bias_ref[h, :, :n_past]
        s_n = lax.dot_general(q, kn_ref[:, sl], nt, preferred_element_type=F32) * ATTN_SCALE + bias_ref[h, :, n_past:]
        m = jnp.maximum(jnp.max(s_c, axis=1, keepdims=True), jnp.max(s_n, axis=1, keepdims=True))
        p_c = jnp.exp(s_c - m)
        p_n = jnp.exp(s_n - m)
        l = jnp.sum(p_c, axis=1, keepdims=True) + jnp.sum(p_n, axis=1, keepdims=True)
        o = jnp.dot(p_c.astype(BF16), vc, preferred_element_type=F32)
        o = o + jnp.dot(p_n.astype(BF16), vn_ref[:, sl], preferred_element_type=F32)
        o_ref[:, sl] = (o / l).astype(o_ref.dtype)


def _band_sample(zq, zkv, cache_k, cache_v, bias_s, row_blk0, dec_batch, t_new):
    n_past = cache_k.shape[1]
    q_col = RS_QB // W_B
    k_col = ST_KB // W_B
    v_col = ST_VB // W_B
    return pl.pallas_call(
        functools.partial(_band_sample_kernel, n_past=n_past),
        grid=(dec_batch,),
        in_specs=[pl.BlockSpec((t_new, W_B), lambda b: (row_blk0 + b, q_col)),
                  pl.BlockSpec((t_new, W_B), lambda b: (row_blk0 + b, k_col)),
                  pl.BlockSpec((t_new, W_B), lambda b: (row_blk0 + b, v_col)),
                  pl.BlockSpec((1, n_past, W_B), lambda b: (b, 0, 0)),
                  pl.BlockSpec((1, n_past, W_B), lambda b: (b, 0, 0)),
                  pl.BlockSpec((H_B, t_new, n_past + t_new), lambda b: (0, 0, 0))],
        out_specs=pl.BlockSpec((t_new, W_B), lambda b: (b, 0)),
        out_shape=jax.ShapeDtypeStruct((dec_batch * t_new, W_B), BF16),
        compiler_params=_cparams(("arbitrary",)),
        name="band_sample",
    )(zq, zkv, zkv, cache_k, cache_v, bias_s)


GLA_EXP_CLAMP = 80.0


def _gla_kernel(q_ref, k_ref, v_ref, r_ref, g_ref, s0_ref, gn_ref, o_ref, so_ref, st_ref, *, chunk):
    n = pl.program_id(1)
    nc = pl.num_programs(1)

    @pl.when(n == 0)
    def _():
        for h in range(H_C):
            st_ref[h] = s0_ref[0, h].T

    ii = lax.broadcasted_iota(jnp.int32, (chunk, chunk), 0)
    jj = lax.broadcasted_iota(jnp.int32, (chunk, chunk), 1)
    tril = jj <= ii
    g_cum = jnp.dot(tril.astype(F32), g_ref[...], preferred_element_type=F32, precision=lax.Precision.HIGHEST)
    nt = (((1,), (1,)), ((), ()))
    tn = (((0,), (0,)), ((), ()))
    for h in range(H_C):
        ks = slice(h * DK_C, (h + 1) * DK_C)
        vs = slice(h * DV_C, (h + 1) * DV_C)
        gc = g_cum[:, ks]
        g_last = gc[chunk - 1:chunk, :]
        qf = q_ref[:, ks].astype(F32) * (DK_C ** -0.5)
        kf = k_ref[:, ks].astype(F32)
        vh = v_ref[:, vs]
        q_dec = (qf * jnp.exp(gc)).astype(BF16)
        k_inv = (kf * jnp.exp(jnp.minimum(-gc, GLA_EXP_CLAMP))).astype(BF16)
        k_dec = (kf * jnp.exp(g_last - gc)).astype(BF16)
        s_t = st_ref[h]
        o = lax.dot_general(q_dec, s_t.astype(BF16), nt, preferred_element_type=F32)
        a = lax.dot_general(q_dec, k_inv, nt, preferred_element_type=F32)
        a = jnp.where(tril, a, 0.0)
        o = o + jnp.dot(a.astype(BF16), vh, preferred_element_type=F32)
        st_ref[h] = s_t * jnp.exp(g_last) + lax.dot_general(vh, k_dec, tn, preferred_element_type=F32)
        on = o * lax.rsqrt(jnp.mean(o * o, axis=-1, keepdims=True) + RMS_EPS) * gn_ref[...]
        o_ref[:, vs] = (on * _silu(r_ref[:, vs].astype(F32))).astype(o_ref.dtype)

    @pl.when(n == nc - 1)
    def _():
        for h in range(H_C):
            so_ref[0, h] = st_ref[h].T


def _gla(zq, g, s0, gnorm, row_blk0, n_seq, n_chunks, *, chunk):
    qc, kc, vc, rc = RS_QC // QK_C, RS_KC // QK_C, RS_VC // W_C, RS_RC // W_C
    row = lambda b, n: row_blk0 + b * n_chunks + n
    return pl.pallas_call(
        functools.partial(_gla_kernel, chunk=chunk),
        grid=(n_seq, n_chunks),
        in_specs=[pl.BlockSpec((chunk, QK_C), lambda b, n: (row(b, n), qc)),
                  pl.BlockSpec((chunk, QK_C), lambda b, n: (row(b, n), kc)),
                  pl.BlockSpec((chunk, W_C), lambda b, n: (row(b, n), vc)),
                  pl.BlockSpec((chunk, W_C), lambda b, n: (row(b, n), rc)),
                  pl.BlockSpec((chunk, QK_C), lambda b, n: (row(b, n), 0)),
                  pl.BlockSpec((1, H_C, DK_C, DV_C), lambda b, n: (b, 0, 0, 0)),
                  pl.BlockSpec((1, DV_C), lambda b, n: (0, 0))],
        out_specs=[pl.BlockSpec((chunk, W_C), lambda b, n: (b * n_chunks + n, 0)),
                   pl.BlockSpec((1, H_C, DK_C, DV_C), lambda b, n: (b, 0, 0, 0))],
        out_shape=[jax.ShapeDtypeStruct((n_seq * n_chunks * chunk, W_C), BF16),
                   jax.ShapeDtypeStruct((n_seq, H_C, DK_C, DV_C), F32)],
        scratch_shapes=[pltpu.VMEM((H_C, DV_C, DK_C), F32)],
        compiler_params=_cparams(("arbitrary", "arbitrary")),
        name="gla",
    )(zq, zq, zq, zq, g, s0, gnorm.reshape(1, DV_C))


def _router_kernel(x_ref, w_ref, o_ref, *, n_experts):
    logits = jnp.dot(x_ref[...], w_ref[...], preferred_element_type=F32, precision=lax.Precision.HIGHEST)
    lane = lax.broadcasted_iota(jnp.int32, logits.shape, 1)
    lg = jnp.where(lane < n_experts, logits, NEG_INF)
    m0 = jnp.max(lg, axis=1, keepdims=True)
    i0 = jnp.min(jnp.where(lg == m0, lane, LANE), axis=1, keepdims=True)
    lg1 = jnp.where(lane == i0, NEG_INF, lg)
    m1 = jnp.max(lg1, axis=1, keepdims=True)
    i1 = jnp.min(jnp.where(lg1 == m1, lane, LANE), axis=1, keepdims=True)
    e1 = jnp.exp(m1 - m0)
    den = 1.0 + e1
    out = jnp.where(lane == 0, i0.astype(F32),
                    jnp.where(lane == 1, i1.astype(F32),
                              jnp.where(lane == 2, 1.0 / den, jnp.where(lane == 3, e1 / den, 0.0))))
    o_ref[...] = out


def _router(x_f32, w_router_pad, n_experts, *, tm):
    n, d = x_f32.shape
    return pl.pallas_call(
        functools.partial(_router_kernel, n_experts=n_experts),
        grid=(n // tm,),
        in_specs=[pl.BlockSpec((tm, d), lambda t: (t, 0)),
                  pl.BlockSpec((d, LANE), lambda t: (0, 0))],
        out_specs=pl.BlockSpec((tm, LANE), lambda t: (t, 0)),
        out_shape=jax.ShapeDtypeStruct((n, LANE), F32),
        compiler_params=_cparams(("arbitrary",)),
        name="moe_router",
    )(x_f32, w_router_pad)


def _gather_kernel(src_ref, x_hbm, o_ref, buf_ref, sem, *, tr):
    def row_copy(r):
        return pltpu.make_async_copy(x_hbm.at[pl.ds(src_ref[r], 1)], buf_ref.at[pl.ds(r, 1)], sem)

    def issue(r, c):
        row_copy(r).start()
        return c

    def drain(r, c):
        row_copy(r).wait()
        return c

    lax.fori_loop(0, tr, issue, 0)
    lax.fori_loop(0, tr, drain, 0)
    o_ref[...] = buf_ref[...].astype(o_ref.dtype)


def _gather_rows(x_f32, src, *, tr):
    d = x_f32.shape[1]
    rows = src.shape[0]
    return pl.pallas_call(
        functools.partial(_gather_kernel, tr=tr),
        grid=(rows // tr,),
        in_specs=[pl.BlockSpec((tr,), lambda t: (t,), memory_space=pltpu.SMEM),
                  pl.BlockSpec(memory_space=pl.ANY)],
        out_specs=pl.BlockSpec((tr, d), lambda t: (t, 0)),
        out_shape=jax.ShapeDtypeStruct((rows, d), BF16),
        scratch_shapes=[pltpu.VMEM((tr, d), F32), pltpu.SemaphoreType.DMA(())],
        compiler_params=_cparams(("arbitrary",)),
        name="moe_gather",
    )(src, x_f32)


def _combine_kernel(s0_ref, s1_ref, y_hbm, x_ref, g_ref, b_ref, of_ref, ob_ref, buf0, buf1, sem, *, tc, alpha):
    def copies(r):
        return (pltpu.make_async_copy(y_hbm.at[pl.ds(s0_ref[r], 1)], buf0.at[pl.ds(r, 1)], sem),
                pltpu.make_async_copy(y_hbm.at[pl.ds(s1_ref[r], 1)], buf1.at[pl.ds(r, 1)], sem))

    def issue(r, c):
        a, b = copies(r)
        a.start()
        b.start()
        return c

    def drain(r, c):
        a, b = copies(r)
        a.wait()
        b.wait()
        return c

    lax.fori_loop(0, tc, issue, 0)
    lax.fori_loop(0, tc, drain, 0)
    y = _layer_norm(alpha * x_ref[...] + (buf0[...] + buf1[...]), g_ref[...], b_ref[...])
    of_ref[...] = y
    ob_ref[...] = y.astype(BF16)


def _moe_combine(ys, slot0, slot1, x_f32, ln_g, ln_b, alpha, *, tc):
    n, d = x_f32.shape
    return pl.pallas_call(
        functools.partial(_combine_kernel, tc=tc, alpha=alpha),
        grid=(n // tc,),
        in_specs=[pl.BlockSpec((tc,), lambda t: (t,), memory_space=pltpu.SMEM),
                  pl.BlockSpec((tc,), lambda t: (t,), memory_space=pltpu.SMEM),
                  pl.BlockSpec(memory_space=pl.ANY),
                  pl.BlockSpec((tc, d), lambda t: (t, 0)),
                  pl.BlockSpec((1, d), lambda t: (0, 0)),
                  pl.BlockSpec((1, d), lambda t: (0, 0))],
        out_specs=[pl.BlockSpec((tc, d), lambda t: (t, 0)), pl.BlockSpec((tc, d), lambda t: (t, 0))],
        out_shape=[jax.ShapeDtypeStruct((n, d), F32), jax.ShapeDtypeStruct((n, d), BF16)],
        scratch_shapes=[pltpu.VMEM((tc, d), F32), pltpu.VMEM((tc, d), F32), pltpu.SemaphoreType.DMA(())],
        compiler_params=_cparams(("arbitrary",)),
        name="moe_combine",
    )(slot0, slot1, ys, x_f32, ln_g.reshape(1, d), ln_b.reshape(1, d))


def _moe_plan(route, n_experts, tm):
    n = route.shape[0]
    e = jnp.concatenate([route[:, 0], route[:, 1]]).astype(jnp.int32)
    gate = jnp.concatenate([route[:, 2], route[:, 3]])
    token = jnp.concatenate([jnp.arange(n, dtype=jnp.int32)] * TOP_K)
    onehot = (e[:, None] == jnp.arange(n_experts, dtype=jnp.int32)[None, :]).astype(jnp.int32)
    csum = jnp.cumsum(onehot, axis=0)
    rank = jnp.take_along_axis(csum, e[:, None], axis=1)[:, 0] - 1
    counts = csum[-1]
    tiles_per = (counts + tm - 1) // tm
    tile_end = jnp.cumsum(tiles_per)
    row_start = (tile_end - tiles_per) * tm
    dest = row_start[e] + rank
    n_tiles = (TOP_K * n + n_experts * (tm - 1)) // tm
    rows = n_tiles * tm
    src = jnp.zeros((rows,), jnp.int32).at[dest].set(token)
    gate_rows = jnp.zeros((rows,), F32).at[dest].set(gate)
    tile_expert = jnp.minimum(
        jnp.searchsorted(tile_end, jnp.arange(n_tiles, dtype=jnp.int32), side="right"), n_experts - 1).astype(jnp.int32)
    n_used = tile_end[-1:].astype(jnp.int32)
    return src, gate_rows.reshape(rows, 1), tile_expert, n_used, dest[:n], dest[n:]


def _ffn_dense(x_f32, x_bf, w1, w3, w2, ln_g, ln_b, alpha, cfg):
    n, d = x_bf.shape
    f = w1.shape[-1]
    tm, tf = cfg["ffn_tm"], cfg["ffn_tf"]
    te = jnp.zeros((n // tm,), jnp.int32)
    nu = jnp.full((1,), n // tm, jnp.int32)
    a, = _matmul_wres(x_bf, [w1[None], w3[None]], te, nu, tm=tm, tn=tf, col0=0, n_cols=f,
                      out_dtypes=[BF16], epilogue=_epi_swiglu, name="ffn_gate_up")
    return _proj_resid_ln(a, w2, x_f32, ln_g, ln_b, alpha, tm=cfg["ln_tm"], tk=cfg["down_tk"], name="ffn_down_ln")


def _ffn_moe(x_f32, x_bf, w_router_pad, n_experts, w1, w3, w2, ln_g, ln_b, alpha, cfg):
    n, d = x_bf.shape
    f = w1.shape[-1]
    tm, tf = cfg["moe_tm"], cfg["ffn_tf"]
    route = _router(x_f32, w_router_pad, n_experts, tm=cfg["row_tm"])
    src, gate_rows, tile_expert, n_used, slot0, slot1 = _moe_plan(route, n_experts, tm)
    xs = _gather_rows(x_f32, src, tr=cfg["gather_tr"])
    a, = _matmul_wres(xs, [w1, w3], tile_expert, n_used, tm=tm, tn=tf, col0=0, n_cols=f,
                      out_dtypes=[BF16], epilogue=_epi_swiglu, name="moe_gate_up")
    ys, = _matmul_acc(a, w2, tile_expert, n_used, [gate_rows],
                      [pl.BlockSpec((tm, 1), lambda t, k, te, nu: (t, 0))],
                      tm=tm, tk=cfg["down_tk"], out_dtypes=[F32], epilogue=_epi_row_scale, name="moe_down")
    return _moe_combine(ys, slot0, slot1, x_f32, ln_g, ln_b, alpha, tc=cfg["combine_tc"])


def _split_w_in(w_in):
    sizes = (W_A, W_A, W_A, H_A, W_B, W_B, W_B, QK_C, QK_C, W_C, W_C, GATE_RANK)
    offs = np.concatenate([[0], np.cumsum(sizes)])
    part = lambda i: w_in[:, offs[i]:offs[i + 1]]
    qa, ka, va, fa, qb, kb, vb, qc, kc, vc, rc, ac = [part(i) for i in range(12)]
    main = jnp.concatenate([ka, va, kb, vb, qc, kc, vc, rc, qa, qb], axis=1).astype(BF16)
    d = w_in.shape[0]
    fa_t = jnp.zeros((16, d), F32).at[:H_A].set(fa.T).astype(BF16)
    ac_pad = jnp.zeros((d, LANE), F32).at[:, :GATE_RANK].set(ac).astype(BF16)
    return main, fa_t, ac_pad


def _mixer(x_bf, dims, caches, params, cfg):
    batch, seq, dec_batch, t_new = dims
    c_fk, c_fv, c_flogf, c_bk, c_bv, s_gla = caches
    w_main, w_fa_t, w_ac_pad, b_forget, rel_bias, w_gate_up, b_gate, gla_norm_g = params
    n = x_bf.shape[0]
    n_p = batch * seq
    past = c_fk.shape[1]
    n_band = c_bk.shape[1]
    assert seq % CHUNK == 0 and t_new == CHUNK and past % CHUNK == 0 and n_band % LANE == 0
    assert n_band == min(BAND_PAST, past) and n_p % t_new == 0

    tm, tn = cfg["proj_tm"], cfg["proj_tn"]
    te = jnp.zeros((n // tm,), jnp.int32)
    nu = jnp.full((1,), n // tm, jnp.int32)
    zs_f32, zs = _matmul_wres(x_bf, [w_main[None]], te, nu, tm=tm, tn=tn, col0=0, n_cols=W_STATE,
                              out_dtypes=[F32, BF16], epilogue=_epi_store, name="proj_state")
    zr, = _matmul_wres(x_bf, [w_main[None]], te, nu, tm=tm, tn=tn, col0=W_STATE, n_cols=W_REST,
                       out_dtypes=[BF16], epilogue=_epi_store, name="proj_rest")

    b_col = jnp.zeros((16, 1), F32).at[:H_A, 0].set(b_forget)
    logf_t = _forget_logits(x_bf, w_fa_t, b_col, tm=cfg["row_tm"])
    w_up_pad = jnp.zeros((LANE, QK_C), F32).at[:GATE_RANK].set(w_gate_up).astype(BF16)
    g = _gla_gates(x_bf, w_ac_pad, w_up_pad, b_gate, tm=cfg["row_tm"])

    c_p = _cumsum_last(logf_t[:, :n_p].reshape(16 * batch, seq)).reshape(16 * batch, 1, seq)
    o_a_p = _fox_prompt(zr, zs, c_p, batch, seq, tq=cfg["fox_tq"])
    logf_s = logf_t[:H_A, n_p:].reshape(H_A, dec_batch, t_new).transpose(1, 0, 2)
    logf_all = jnp.concatenate([c_flogf.transpose(0, 2, 1), logf_s], axis=2)
    c_s = _cumsum_last(logf_all.reshape(dec_batch * H_A, past + t_new))
    hg = cfg["fox_hg"]
    c_s = c_s.reshape(dec_batch * (H_A // hg), hg, past + t_new)
    o_a_s = _fox_sample(zr, zs, c_fk.reshape(dec_batch, past, W_A), c_fv.reshape(dec_batch, past, W_A),
                        c_s, n_p // t_new, dec_batch, t_new, hg=hg)

    bias_p, bias_s = _band_bias(rel_bias, tq=cfg["band_tq"], t_new=t_new, n_past=n_band)
    o_b_p = _band_prompt(zr, zs, bias_p, batch, seq, tq=cfg["band_tq"])
    o_b_s = _band_sample(zr, zs, c_bk.reshape(dec_batch, n_band, W_B), c_bv.reshape(dec_batch, n_band, W_B),
                         bias_s, n_p // t_new, dec_batch, t_new)

    s_zero = jnp.zeros((batch, H_C, DK_C, DV_C), F32)
    o_c_p, s_p = _gla(zr, g, s_zero, gla_norm_g, 0, batch, seq // CHUNK, chunk=CHUNK)
    o_c_s, s_s = _gla(zr, g, s_gla, gla_norm_g, n_p // CHUNK, dec_batch, 1, chunk=CHUNK)

    o_cat = jnp.concatenate([jnp.concatenate([o_a_p, o_b_p, o_c_p], axis=1),
                             jnp.concatenate([o_a_s, o_b_s, o_c_s], axis=1)], axis=0)

    ka, va = zs_f32[:, ST_KA:ST_KA + W_A], zs_f32[:, ST_VA:ST_VA + W_A]
    kb, vb = zs_f32[:, ST_KB:ST_KB + W_B], zs_f32[:, ST_VB:ST_VB + W_B]
    heads = lambda t, b, l, h: t.reshape(b, l, h, HEAD_DIM)
    nbp = min(BAND_PAST, seq)
    logf_p = logf_t[:H_A, :n_p].reshape(H_A, batch, seq).transpose(1, 2, 0)
    st_p = (heads(ka[:n_p], batch, seq, H_A), heads(va[:n_p], batch, seq, H_A), logf_p,
            heads(kb[:n_p], batch, seq, H_B)[:, seq - nbp:], heads(vb[:n_p], batch, seq, H_B)[:, seq - nbp:], s_p)
    kb_s, vb_s = heads(kb[n_p:], dec_batch, t_new, H_B), heads(vb[n_p:], dec_batch, t_new, H_B)
    st_s = (heads(ka[n_p:], dec_batch, t_new, H_A), heads(va[n_p:], dec_batch, t_new, H_A),
            logf_s.transpose(0, 2, 1),
            jnp.concatenate([c_bk, kb_s], axis=1)[:, t_new:], jnp.concatenate([c_bv, vb_s], axis=1)[:, t_new:], s_s)
    return o_cat, st_p, st_s


def _config(n, d, f, seq):
    return dict(
        proj_tm=_tile(n, 1024, 256), proj_tn=_tile(W_STATE, 1024, 512),
        row_tm=_tile(n, 512, 256),
        ffn_tm=_tile(n, 1024, 256), ffn_tf=_tile(f, 512, 128),
        ln_tm=_tile(n, 256, 128), down_tk=_tile(f, 512, 128), out_tk=_tile(4096, 512, 128),
        moe_tm=512 if n >= 4096 else 128, gather_tr=256 if n >= 4096 else 128,
        combine_tc=_tile(n, 256, 128),
        fox_tq=_tile(seq, 512, 128), fox_hg=4, band_tq=_tile(seq, 512, 128),
    )


def kernel(x_prompt, x_sample, cache_fox_k, cache_fox_v, cache_fox_logf, cache_band_k, cache_band_v, state_gla,
           w_in, b_forget, rel_bias, w_gate_up, b_gate, gla_norm_g, w_out,
           ln1_g, ln1_b, ln2_g, ln2_b, ffn_w1, ffn_w3, ffn_w2,
           moe_router, moe_w1, moe_w3, moe_w2):
    batch, seq, d = x_prompt.shape
    dec_batch, t_new, _ = x_sample.shape
    depth = w_in.shape[0]
    f = ffn_w1.shape[-1]
    n_experts = moe_router.shape[-1]
    n_p, n_s = batch * seq, dec_batch * t_new
    n = n_p + n_s
    alpha = (2.0 * depth) ** 0.25
    cfg = _config(n, d, f, seq)
    dims = (batch, seq, dec_batch, t_new)

    x = jnp.concatenate([x_prompt.reshape(n_p, d), x_sample.reshape(n_s, d)], axis=0)
    x_bf = x.astype(BF16)
    sp_all, ss_all = [], []
    for i in range(depth):
        w_main, w_fa_t, w_ac_pad = _split_w_in(w_in[i])
        caches = (cache_fox_k[i], cache_fox_v[i], cache_fox_logf[i], cache_band_k[i], cache_band_v[i], state_gla[i])
        params = (w_main, w_fa_t, w_ac_pad, b_forget[i], rel_bias[i], w_gate_up[i], b_gate[i], gla_norm_g[i])
        o_cat, st_p, st_s = _mixer(x_bf, dims, caches, params, cfg)
        sp_all.append(st_p)
        ss_all.append(st_s)
        x, x_bf = _proj_resid_ln(o_cat, w_out[i].astype(BF16), x, ln1_g[i], ln1_b[i], alpha,
                                 tm=cfg["ln_tm"], tk=cfg["out_tk"], name="out_proj_ln")
        j = i // 2
        if i % 2 == 0:
            x, x_bf = _ffn_dense(x, x_bf, ffn_w1[j].astype(BF16), ffn_w3[j].astype(BF16), ffn_w2[j].astype(BF16),
                                 ln2_g[i], ln2_b[i], alpha, cfg)
        else:
            w_router_pad = jnp.zeros((d, LANE), F32).at[:, :n_experts].set(moe_router[j])
            x, x_bf = _ffn_moe(x, x_bf, w_router_pad, n_experts, moe_w1[j].astype(BF16), moe_w3[j].astype(BF16),
                               moe_w2[j].astype(BF16), ln2_g[i], ln2_b[i], alpha, cfg)
    stack = lambda states, k: jnp.stack([s[k] for s in states], axis=0)
    y_p = x[:n_p].reshape(batch, seq, d)
    y_s = x[n_p:].reshape(dec_batch, t_new, d)
    return (y_p, y_s) + tuple(stack(sp_all, k) for k in range(6)) + tuple(stack(ss_all, k) for k in range(6))
```

```python
import functools
import math

import jax
import jax.numpy as jnp
import numpy as np
from jax import lax
from jax.experimental import pallas as pl
from jax.experimental.pallas import tpu as pltpu

CHUNK = 64
HEAD_DIM = 128
H_A = 12
H_B = 8
H_C = 6
DK_C = 128
DV_C = 256
W_A = H_A * HEAD_DIM
W_B = H_B * HEAD_DIM
QK_C = H_C * DK_C
W_C = H_C * DV_C
GATE_RANK = 16
GLA_TAU = 16.0
BAND_CHUNKS = 8
BAND_PAST = BAND_CHUNKS * CHUNK
MAX_REL = 128
TOP_K = 2
LN_EPS = 1e-5
RMS_EPS = 1e-6
ATTN_SCALE = HEAD_DIM ** -0.5

LANE = 128
NEG_INF = float("-inf")
F32 = jnp.float32
BF16 = jnp.bfloat16
VMEM_LIMIT = 56 * 1024 * 1024

ST_KA, ST_VA, ST_KB, ST_VB = 0, W_A, 2 * W_A, 2 * W_A + W_B
W_STATE = 2 * W_A + 2 * W_B
RS_QC, RS_KC, RS_VC, RS_RC, RS_QA, RS_QB = 0, QK_C, 2 * QK_C, 2 * QK_C + W_C, 2 * QK_C + 2 * W_C, 2 * QK_C + 2 * W_C + W_A
W_REST = RS_QB + W_B
OC_C, OC_A, OC_B = 0, W_C, W_C + W_A
MIX_WIDTH = W_A + W_B + W_C


def _tile(dim, pref, mult):
    if dim <= pref:
        return dim
    t = (pref // mult) * mult
    while t > mult and dim % t:
        t -= mult
    assert dim % t == 0, (dim, pref, mult)
    return t


def _cparams(sem):
    return pltpu.CompilerParams(dimension_semantics=sem, vmem_limit_bytes=VMEM_LIMIT)


def _carry_output(body):
    def wrapped(_, *refs):
        body(*refs)
    return wrapped


def _carried(o_cat, body):
    return _carry_output(body), [pl.BlockSpec(memory_space=pl.ANY)], [o_cat], {0: 0}


def _log_sigmoid(x):
    return jnp.minimum(x, 0.0) - jnp.log(1.0 + jnp.exp(-jnp.abs(x)))


def _silu(x):
    return x / (1.0 + jnp.exp(-x))


def _layer_norm(x, g, b):
    mu = jnp.mean(x, axis=-1, keepdims=True)
    xc = x - mu
    var = jnp.mean(xc * xc, axis=-1, keepdims=True)
    return xc * lax.rsqrt(var + LN_EPS) * g + b


NT_DIMS = (((1,), (1,)), ((), ()))


def _wres_kernel(te_ref, nu_ref, x_ref, *refs, n_w, epilogue, w_transposed):
    w_refs = refs[:n_w]
    out_refs = refs[n_w:]
    t = pl.program_id(1)

    @pl.when(t < nu_ref[0])
    def _():
        x = x_ref[...]
        if w_transposed:
            accs = [lax.dot_general(x, w[0], NT_DIMS, preferred_element_type=F32) for w in w_refs]
        else:
            accs = [jnp.dot(x, w[0], preferred_element_type=F32) for w in w_refs]
        epilogue(accs, out_refs)

    @pl.when(t >= nu_ref[0])
    def _():
        for o in out_refs:
            o[...] = jnp.zeros(o.shape, o.dtype)


def _matmul_wres(x, ws, tile_expert, n_used, *, tm, tn, col0, n_cols, out_dtypes, epilogue, name,
                 w_transposed=False):
    rows, kdim = x.shape
    nt = rows // tm
    nj = n_cols // tn
    j0 = col0 // tn
    assert rows % tm == 0 and n_cols % tn == 0 and col0 % tn == 0
    if w_transposed:
        w_spec = pl.BlockSpec((1, tn, kdim), lambda j, t, te, nu: (te[t], j + j0, 0))
    else:
        w_spec = pl.BlockSpec((1, kdim, tn), lambda j, t, te, nu: (te[t], 0, j + j0))
    grid_spec = pltpu.PrefetchScalarGridSpec(
        num_scalar_prefetch=2,
        grid=(nj, nt),
        in_specs=[pl.BlockSpec((tm, kdim), lambda j, t, te, nu: (t, 0))] + [w_spec for _ in ws],
        out_specs=[pl.BlockSpec((tm, tn), lambda j, t, te, nu: (t, j)) for _ in out_dtypes],
    )
    return pl.pallas_call(
        functools.partial(_wres_kernel, n_w=len(ws), epilogue=epilogue, w_transposed=w_transposed),
        grid_spec=grid_spec,
        out_shape=[jax.ShapeDtypeStruct((rows, n_cols), dt) for dt in out_dtypes],
        compiler_params=_cparams(("arbitrary", "arbitrary")),
        name=name,
    )(tile_expert, n_used, x, *ws)


def _epi_store(accs, outs):
    for o in outs:
        o[...] = accs[0].astype(o.dtype)


def _epi_swiglu(accs, outs):
    outs[0][...] = (_silu(accs[0]) * accs[1]).astype(outs[0].dtype)


ROW_SLAB = 32


def _acc_kernel(te_ref, nu_ref, a_ref, w_ref, *refs, n_extra, epilogue):
    extra = refs[:n_extra]
    outs = refs[n_extra:]
    acc_ref = outs[0]
    t = pl.program_id(0)
    k = pl.program_id(1)
    nk = pl.num_programs(1)
    used = t < nu_ref[0]

    @pl.when(used & (k == 0))
    def _():
        acc_ref[...] = jnp.dot(a_ref[...], w_ref[0], preferred_element_type=F32)

    @pl.when(used & (k > 0))
    def _():
        acc_ref[...] += jnp.dot(a_ref[...], w_ref[0], preferred_element_type=F32)

    @pl.when(used & (k == nk - 1))
    def _():
        def slab(i, c):
            rows = pl.ds(pl.multiple_of(i * ROW_SLAB, ROW_SLAB), ROW_SLAB)
            epilogue(rows, extra, outs)
            return c

        lax.fori_loop(0, acc_ref.shape[0] // ROW_SLAB, slab, 0)

    @pl.when(jnp.logical_not(used) & (k == nk - 1))
    def _():
        for o in outs:
            o[...] = jnp.zeros(o.shape, o.dtype)


def _matmul_acc(a, w, tile_expert, n_used, extra, extra_specs, *, tm, tk, out_dtypes, epilogue, name):
    rows, kdim = a.shape
    n_out = w.shape[2]
    nt = rows // tm
    nk = kdim // tk
    assert rows % tm == 0 and kdim % tk == 0 and tm % ROW_SLAB == 0 and out_dtypes[0] == F32
    grid_spec = pltpu.PrefetchScalarGridSpec(
        num_scalar_prefetch=2,
        grid=(nt, nk),
        in_specs=[pl.BlockSpec((tm, tk), lambda t, k, te, nu: (t, k)),
                  pl.BlockSpec((1, tk, n_out), lambda t, k, te, nu: (te[t], k, 0))] + extra_specs,
        out_specs=[pl.BlockSpec((tm, n_out), lambda t, k, te, nu: (t, 0)) for _ in out_dtypes],
    )
    return pl.pallas_call(
        functools.partial(_acc_kernel, n_extra=len(extra), epilogue=epilogue),
        grid_spec=grid_spec,
        out_shape=[jax.ShapeDtypeStruct((rows, n_out), dt) for dt in out_dtypes],
        compiler_params=_cparams(("arbitrary", "arbitrary")),
        name=name,
    )(tile_expert, n_used, a, w, *extra)


def _epi_resid_ln(rows, extra, outs, *, alpha):
    x_ref, g_ref, b_ref = extra
    y = _layer_norm(alpha * x_ref[rows, :] + outs[0][rows, :], g_ref[...], b_ref[...])
    outs[0][rows, :] = y
    outs[1][rows, :] = y.astype(BF16)


def _epi_row_scale(rows, extra, outs):
    outs[0][rows, :] = outs[0][rows, :] * extra[0][rows, :]


def _proj_resid_ln(a, w, x_f32, ln_g, ln_b, alpha, *, tm, tk, name):
    rows = a.shape[0]
    d = w.shape[-1]
    te = jnp.zeros((rows // tm,), jnp.int32)
    nu = jnp.full((1,), rows // tm, jnp.int32)
    extra_specs = [pl.BlockSpec((tm, d), lambda t, k, te, nu: (t, 0), pipeline_mode=pl.Buffered(1)),
                   pl.BlockSpec((1, d), lambda t, k, te, nu: (0, 0)),
                   pl.BlockSpec((1, d), lambda t, k, te, nu: (0, 0))]
    return _matmul_acc(a, w.reshape((1,) + w.shape[-2:]), te, nu,
                       [x_f32, ln_g.reshape(1, d), ln_b.reshape(1, d)], extra_specs,
                       tm=tm, tk=tk, out_dtypes=[F32, BF16],
                       epilogue=functools.partial(_epi_resid_ln, alpha=alpha), name=name)


def _forget_kernel(wt_ref, b_ref, x_ref, o_ref):
    fa = lax.dot_general(wt_ref[...], x_ref[...], (((1,), (1,)), ((), ())), preferred_element_type=F32)
    o_ref[...] = _log_sigmoid(fa + b_ref[...])


def _forget_logits(x_bf, w_fa_t, b_col, *, tm):
    n, d = x_bf.shape
    hp = w_fa_t.shape[0]
    return pl.pallas_call(
        _forget_kernel,
        grid=(n // tm,),
        in_specs=[pl.BlockSpec((hp, d), lambda t: (0, 0)),
                  pl.BlockSpec((hp, 1), lambda t: (0, 0)),
                  pl.BlockSpec((tm, d), lambda t: (t, 0))],
        out_specs=pl.BlockSpec((hp, tm), lambda t: (0, t)),
        out_shape=jax.ShapeDtypeStruct((hp, n), F32),
        compiler_params=_cparams(("arbitrary",)),
        name="forget_logits",
    )(w_fa_t, b_col, x_bf)


def _gate_kernel(x_ref, wa_ref, wu_ref, b_ref, o_ref):
    ac = lax.dot_general(x_ref[...], wa_ref[...], NT_DIMS, preferred_element_type=F32)
    pre = jnp.dot(ac.astype(BF16), wu_ref[...], preferred_element_type=F32) + b_ref[...]
    o_ref[...] = _log_sigmoid(pre) * (1.0 / GLA_TAU)


def _gla_gates(x_bf, w_ac_pad, w_up_pad, b_gate, *, tm):
    n, d = x_bf.shape
    return pl.pallas_call(
        _gate_kernel,
        grid=(n // tm,),
        in_specs=[pl.BlockSpec((tm, d), lambda t: (t, 0)),
                  pl.BlockSpec((LANE, d), lambda t: (0, 0)),
                  pl.BlockSpec((LANE, QK_C), lambda t: (0, 0)),
                  pl.BlockSpec((1, QK_C), lambda t: (0, 0))],
        out_specs=pl.BlockSpec((tm, QK_C), lambda t: (t, 0)),
        out_shape=jax.ShapeDtypeStruct((n, QK_C), F32),
        compiler_params=_cparams(("arbitrary",)),
        name="gla_gates",
    )(x_bf, w_ac_pad, w_up_pad, b_gate.reshape(1, QK_C))


def _cumsum_kernel(x_ref, o_ref, *, t_len):
    rows = x_ref.shape[0]
    ii = lax.broadcasted_iota(jnp.int32, (LANE, LANE), 0)
    jj = lax.broadcasted_iota(jnp.int32, (LANE, LANE), 1)
    upper = (ii <= jj).astype(F32)
    carry = jnp.zeros((rows, 1), F32)
    for c0 in range(0, t_len, LANE):
        w = min(LANE, t_len - c0)
        blk = x_ref[:, c0:c0 + w]
        cs = jnp.dot(blk, upper[:w, :w], preferred_element_type=F32, precision=lax.Precision.HIGHEST) + carry
        o_ref[:, c0:c0 + w] = cs
        carry = cs[:, w - 1:w]


def _cumsum_last(x):
    rows, t_len = x.shape
    rb = 8
    assert rows % rb == 0
    return pl.pallas_call(
        functools.partial(_cumsum_kernel, t_len=t_len),
        grid=(rows // rb,),
        in_specs=[pl.BlockSpec((rb, t_len), lambda r: (r, 0))],
        out_specs=pl.BlockSpec((rb, t_len), lambda r: (r, 0)),
        out_shape=jax.ShapeDtypeStruct((rows, t_len), F32),
        compiler_params=_cparams(("arbitrary",)),
        name="cumsum_time",
    )(x)


def _fox_prompt_kernel(q_ref, k_ref, v_ref, c_ref, o_ref, *, tq):
    qi = pl.program_id(2)
    q = q_ref[...]
    q0 = pl.multiple_of(qi * tq, tq)
    c_first = c_ref[0, :, pl.ds(q0, LANE)][:, :1]

    def step(k0, carry, masked):
        m, l, acc = carry
        kb = k_ref[pl.ds(k0, tq), :]
        vb = v_ref[pl.ds(k0, tq), :]
        s = lax.dot_general(q, kb, (((1,), (1,)), ((), ())), preferred_element_type=F32) * ATTN_SCALE
        s = s + (c_first - c_ref[0, :, pl.ds(k0, tq)])
        if masked:
            ii = lax.broadcasted_iota(jnp.int32, (tq, tq), 0)
            jj = lax.broadcasted_iota(jnp.int32, (tq, tq), 1)
            s = jnp.where(jj <= ii, s, NEG_INF)
        m_new = jnp.maximum(m, jnp.max(s, axis=1, keepdims=True))
        a = jnp.exp(m - m_new)
        p = jnp.exp(s - m_new)
        l = a * l + jnp.sum(p, axis=1, keepdims=True)
        acc = a * acc + jnp.dot(p.astype(BF16), vb, preferred_element_type=F32)
        return m_new, l, acc

    init = (jnp.full((tq, 1), NEG_INF, F32), jnp.zeros((tq, 1), F32), jnp.zeros((tq, HEAD_DIM), F32))
    carry = lax.fori_loop(0, qi, lambda kj, c: step(pl.multiple_of(kj * tq, tq), c, False), init)
    m, l, acc = step(q0, carry, True)
    o_ref[...] = (acc / l).astype(o_ref.dtype)


def _fox_prompt(o_cat, zq, zkv, c_rows, batch, seq, *, tq):
    nq = seq // tq
    q_col = RS_QA // HEAD_DIM
    k_col = ST_KA // HEAD_DIM
    v_col = ST_VA // HEAD_DIM
    o_col = OC_A // HEAD_DIM
    body, lead_specs, lead_args, aliases = _carried(o_cat, functools.partial(_fox_prompt_kernel, tq=tq))
    return pl.pallas_call(
        body,
        grid=(batch, H_A, nq),
        in_specs=lead_specs + [
            pl.BlockSpec((tq, HEAD_DIM), lambda b, h, i: (b * nq + i, q_col + h)),
            pl.BlockSpec((seq, HEAD_DIM), lambda b, h, i: (b, k_col + h)),
            pl.BlockSpec((seq, HEAD_DIM), lambda b, h, i: (b, v_col + h)),
            pl.BlockSpec((1, 1, seq), lambda b, h, i: (h * batch + b, 0, 0))],
        out_specs=pl.BlockSpec((tq, HEAD_DIM), lambda b, h, i: (b * nq + i, o_col + h)),
        out_shape=jax.ShapeDtypeStruct((zq.shape[0], MIX_WIDTH), BF16),
        input_output_aliases=aliases,
        compiler_params=_cparams(("arbitrary", "arbitrary", "arbitrary")),
        name="fox_prompt",
    )(*lead_args, zq, zkv, zkv, c_rows)


def _fox_sample_kernel(q_ref, kn_ref, vn_ref, ck_ref, cv_ref, c_ref, o_ref, *, hg, past, t_new):
    ii = lax.broadcasted_iota(jnp.int32, (t_new, t_new), 0)
    jj = lax.broadcasted_iota(jnp.int32, (t_new, t_new), 1)
    causal = jj <= ii
    for hh in range(hg):
        sl = slice(hh * HEAD_DIM, (hh + 1) * HEAD_DIM)
        q = q_ref[:, sl]
        kc = ck_ref[hh].astype(BF16)
        vc = cv_ref[hh].astype(BF16)
        kn = kn_ref[:, sl]
        vn = vn_ref[:, sl]
        c_row = c_ref[0, hh:hh + 1, :]
        c_first = c_row[:, past:past + 1]
        s_c = lax.dot_general(q, kc, (((1,), (1,)), ((), ())), preferred_element_type=F32) * ATTN_SCALE
        s_c = s_c + (c_first - c_row[:, :past])
        s_n = lax.dot_general(q, kn, (((1,), (1,)), ((), ())), preferred_element_type=F32) * ATTN_SCALE
        s_n = jnp.where(causal, s_n + (c_first - c_row[:, past:]), NEG_INF)
        m = jnp.maximum(jnp.max(s_c, axis=1, keepdims=True), jnp.max(s_n, axis=1, keepdims=True))
        p_c = jnp.exp(s_c - m)
        p_n = jnp.exp(s_n - m)
        l = jnp.sum(p_c, axis=1, keepdims=True) + jnp.sum(p_n, axis=1, keepdims=True)
        o = jnp.dot(p_c.astype(BF16), vc, preferred_element_type=F32)
        o = o + jnp.dot(p_n.astype(BF16), vn, preferred_element_type=F32)
        o_ref[:, sl] = (o / l).astype(o_ref.dtype)


def _fox_sample(o_cat, zq, zkv, cache_k, cache_v, cache_b0, c_all, row_blk0, dec_batch, t_new, *, hg):
    past = cache_k.shape[1]
    ng = H_A // hg
    wg = hg * HEAD_DIM
    q_col = RS_QA // wg
    k_col = ST_KA // wg
    v_col = ST_VA // wg
    o_col = OC_A // wg
    body, lead_specs, lead_args, aliases = _carried(
        o_cat, functools.partial(_fox_sample_kernel, hg=hg, past=past, t_new=t_new))
    return pl.pallas_call(
        body,
        grid=(dec_batch, ng),
        in_specs=lead_specs + [
            pl.BlockSpec((t_new, wg), lambda b, g: (row_blk0 + b, q_col + g)),
            pl.BlockSpec((t_new, wg), lambda b, g: (row_blk0 + b, k_col + g)),
            pl.BlockSpec((t_new, wg), lambda b, g: (row_blk0 + b, v_col + g)),
            pl.BlockSpec((hg, past, HEAD_DIM), lambda b, g: ((cache_b0 + b) * ng + g, 0, 0)),
            pl.BlockSpec((hg, past, HEAD_DIM), lambda b, g: ((cache_b0 + b) * ng + g, 0, 0)),
            pl.BlockSpec((1, hg, past + t_new), lambda b, g: (b * ng + g, 0, 0))],
        out_specs=pl.BlockSpec((t_new, wg), lambda b, g: (row_blk0 + b, o_col + g)),
        out_shape=jax.ShapeDtypeStruct((zq.shape[0], MIX_WIDTH), BF16),
        input_output_aliases=aliases,
        compiler_params=_cparams(("arbitrary", "arbitrary")),
        name="fox_sample",
    )(*lead_args, zq, zkv, zkv, cache_k, cache_v, c_all)


def _band_bias_kernel(rb_ref, op_ref, os_ref, *, tq, t_new, n_past):
    h = pl.program_id(0)
    ii = lax.broadcasted_iota(jnp.int32, (LANE, LANE), 0)
    jj = lax.broadcasted_iota(jnp.int32, (LANE, LANE), 1)
    idx_same = jj - ii + MAX_REL
    idx_prev = jnp.maximum(jj - ii - LANE, -MAX_REL) + MAX_REL

    def body(r, carry):
        t_same, t_prev = carry
        val = rb_ref[r * H_B + h]
        return jnp.where(idx_same == r, val, t_same), jnp.where(idx_prev == r, val, t_prev)

    zeros = jnp.zeros((LANE, LANE), F32)
    t_same, t_prev = lax.fori_loop(0, 2 * MAX_REL + 1, body, (zeros, zeros))
    far = jnp.full((LANE, LANE), rb_ref[h], F32)
    neg = jnp.full((LANE, LANE), NEG_INF, F32)

    def block(rel_blk):
        return t_same if rel_blk == 0 else t_prev if rel_blk == -1 else far

    for bi in range(tq // LANE):
        for bj in range(2 * tq // LANE):
            koff = bj * LANE - tq
            rel_blk = koff // LANE - bi
            qc = (bi * LANE + ii) // CHUNK
            kc = (koff + jj + tq) // CHUNK - tq // CHUNK
            vis = (kc <= qc) & (kc >= qc - BAND_CHUNKS)
            tile = neg if rel_blk > 0 else jnp.where(vis, block(rel_blk), NEG_INF)
            op_ref[0, bi * LANE:(bi + 1) * LANE, bj * LANE:(bj + 1) * LANE] = tile

    for j0 in range(0, n_past + t_new, LANE):
        w = min(LANE, n_past + t_new - j0)
        rel_blk = (j0 - n_past) // LANE
        qc = ii // CHUNK
        kc = (j0 + jj) // CHUNK - n_past // CHUNK
        vis = (kc <= qc) & (kc >= qc - BAND_CHUNKS)
        tile = jnp.where(vis, block(rel_blk), NEG_INF)
        os_ref[0, :, j0:j0 + w] = tile[:t_new, :w]


def _band_bias(rel_bias, *, tq, t_new, n_past):
    assert tq % LANE == 0 and n_past % LANE == 0 and t_new <= LANE and LANE % CHUNK == 0
    assert MAX_REL == LANE
    return pl.pallas_call(
        functools.partial(_band_bias_kernel, tq=tq, t_new=t_new, n_past=n_past),
        grid=(H_B,),
        in_specs=[pl.BlockSpec(memory_space=pltpu.SMEM)],
        out_specs=[pl.BlockSpec((1, tq, 2 * tq), lambda h: (h, 0, 0)),
                   pl.BlockSpec((1, t_new, n_past + t_new), lambda h: (h, 0, 0))],
        out_shape=[jax.ShapeDtypeStruct((H_B, tq, 2 * tq), F32),
                   jax.ShapeDtypeStruct((H_B, t_new, n_past + t_new), F32)],
        compiler_params=_cparams(("arbitrary",)),
        name="band_bias",
    )(rel_bias.reshape(-1))


def _band_prompt_kernel(q_ref, k_ref, v_ref, bias_ref, o_ref, *, tq):
    t = pl.program_id(2)
    q = q_ref[...]
    lo = pl.multiple_of(jnp.maximum(t - 1, 0) * tq, tq)
    hi = pl.multiple_of(t * tq, tq)
    nt = (((1,), (1,)), ((), ()))
    s_l = lax.dot_general(q, k_ref[pl.ds(lo, tq), :], nt, preferred_element_type=F32) * ATTN_SCALE
    s_l = jnp.where(t > 0, s_l + bias_ref[0, :, :tq], NEG_INF)
    s_r = lax.dot_general(q, k_ref[pl.ds(hi, tq), :], nt, preferred_element_type=F32) * ATTN_SCALE
    s_r = s_r + bias_ref[0, :, tq:]
    m = jnp.maximum(jnp.max(s_l, axis=1, keepdims=True), jnp.max(s_r, axis=1, keepdims=True))
    p_l = jnp.exp(s_l - m)
    p_r = jnp.exp(s_r - m)
    l = jnp.sum(p_l, axis=1, keepdims=True) + jnp.sum(p_r, axis=1, keepdims=True)
    o = jnp.dot(p_l.astype(BF16), v_ref[pl.ds(lo, tq), :], preferred_element_type=F32)
    o = o + jnp.dot(p_r.astype(BF16), v_ref[pl.ds(hi, tq), :], preferred_element_type=F32)
    o_ref[...] = (o / l).astype(o_ref.dtype)


def _band_prompt(o_cat, zq, zkv, bias_p, batch, seq, *, tq):
    nq = seq // tq
    q_col = RS_QB // HEAD_DIM
    k_col = ST_KB // HEAD_DIM
    v_col = ST_VB // HEAD_DIM
    o_col = OC_B // HEAD_DIM
    body, lead_specs, lead_args, aliases = _carried(o_cat, functools.partial(_band_prompt_kernel, tq=tq))
    return pl.pallas_call(
        body,
        grid=(H_B, batch, nq),
        in_specs=lead_specs + [
            pl.BlockSpec((tq, HEAD_DIM), lambda h, b, i: (b * nq + i, q_col + h)),
            pl.BlockSpec((seq, HEAD_DIM), lambda h, b, i: (b, k_col + h)),
            pl.BlockSpec((seq, HEAD_DIM), lambda h, b, i: (b, v_col + h)),
            pl.BlockSpec((1, tq, 2 * tq), lambda h, b, i: (h, 0, 0))],
        out_specs=pl.BlockSpec((tq, HEAD_DIM), lambda h, b, i: (b * nq + i, o_col + h)),
        out_shape=jax.ShapeDtypeStruct((zq.shape[0], MIX_WIDTH), BF16),
        input_output_aliases=aliases,
        compiler_params=_cparams(("arbitrary", "arbitrary", "arbitrary")),
        name="band_prompt",
    )(*lead_args, zq, zkv, zkv, bias_p)


def _band_sample_kernel(q_ref, kn_ref, vn_ref, ck_ref, cv_ref, bias_ref, o_ref, *, n_past):
    nt = (((1,), (1,)), ((), ()))
    for h in range(H_B):
        sl = slice(h * HEAD_DIM, (h + 1) * HEAD_DIM)
        q = q_ref[:, sl]
        kc = ck_ref[0, :, sl].astype(BF16)
        vc = cv_ref[0, :, sl].astype(BF16)
        s_c = lax.dot_general(q, kc, nt, preferred_element_type=F32) * ATTN_SCALE + bias_ref[h, :, :n_past]
        s_n = lax.dot_general(q, kn_ref[:, sl], nt, preferred_element_type=F32) * ATTN_SCALE + bias_ref[h, :, n_past:]
        m = jnp.maximum(jnp.max(s_c, axis=1, keepdims=True), jnp.max(s_n, axis=1, keepdims=True))
        p_c = jnp.exp(s_c - m)
        p_n = jnp.exp(s_n - m)
        l = jnp.sum(p_c, axis=1, keepdims=True) + jnp.sum(p_n, axis=1, keepdims=True)
        o = jnp.dot(p_c.astype(BF16), vc, preferred_element_type=F32)
        o = o + jnp.dot(p_n.astype(BF16), vn_ref[:, sl], preferred_element_type=F32)
        o_ref[:, sl] = (o / l).astype(o_ref.dtype)


def _band_sample(o_cat, zq, zkv, cache_k, cache_v, cache_b0, bias_s, row_blk0, dec_batch, t_new):
    n_past = cache_k.shape[1]
    q_col = RS_QB // W_B
    k_col = ST_KB // W_B
    v_col = ST_VB // W_B
    o_col = OC_B // W_B
    body, lead_specs, lead_args, aliases = _carried(o_cat, functools.partial(_band_sample_kernel, n_past=n_past))
    return pl.pallas_call(
        body,
        grid=(dec_batch,),
        in_specs=lead_specs + [
            pl.BlockSpec((t_new, W_B), lambda b: (row_blk0 + b, q_col)),
            pl.BlockSpec((t_new, W_B), lambda b: (row_blk0 + b, k_col)),
            pl.BlockSpec((t_new, W_B), lambda b: (row_blk0 + b, v_col)),
            pl.BlockSpec((1, n_past, W_B), lambda b: (cache_b0 + b, 0, 0)),
            pl.BlockSpec((1, n_past, W_B), lambda b: (cache_b0 + b, 0, 0)),
            pl.BlockSpec((H_B, t_new, n_past + t_new), lambda b: (0, 0, 0))],
        out_specs=pl.BlockSpec((t_new, W_B), lambda b: (row_blk0 + b, o_col)),
        out_shape=jax.ShapeDtypeStruct((zq.shape[0], MIX_WIDTH), BF16),
        input_output_aliases=aliases,
        compiler_params=_cparams(("arbitrary",)),
        name="band_sample",
    )(*lead_args, zq, zkv, zkv, cache_k, cache_v, bias_s)


GLA_EXP_CLAMP = 80.0


def _gla_kernel(q_ref, k_ref, v_ref, r_ref, g_ref, s0_ref, gn_ref, o_ref, so_ref, st_ref, *, chunk):
    n = pl.program_id(1)
    nc = pl.num_programs(1)

    @pl.when(n == 0)
    def _():
        for h in range(H_C):
            st_ref[h] = s0_ref[0, h].T

    ii = lax.broadcasted_iota(jnp.int32, (chunk, chunk), 0)
    jj = lax.broadcasted_iota(jnp.int32, (chunk, chunk), 1)
    tril = jj <= ii
    g_cum = jnp.dot(tril.astype(F32), g_ref[...], preferred_element_type=F32, precision=lax.Precision.HIGHEST)
    nt = (((1,), (1,)), ((), ()))
    tn = (((0,), (0,)), ((), ()))
    for h in range(H_C):
        ks = slice(h * DK_C, (h + 1) * DK_C)
        vs = slice(h * DV_C, (h + 1) * DV_C)
        gc = g_cum[:, ks]
        g_last = gc[chunk - 1:chunk, :]
        qf = q_ref[:, ks].astype(F32) * (DK_C ** -0.5)
        kf = k_ref[:, ks].astype(F32)
        vh = v_ref[:, vs]
        q_dec = (qf * jnp.exp(gc)).astype(BF16)
        k_inv = (kf * jnp.exp(jnp.minimum(-gc, GLA_EXP_CLAMP))).astype(BF16)
        k_dec = (kf * jnp.exp(g_last - gc)).astype(BF16)
        s_t = st_ref[h]
        o = lax.dot_general(q_dec, s_t.astype(BF16), nt, preferred_element_type=F32)
        a = lax.dot_general(q_dec, k_inv, nt, preferred_element_type=F32)
        a = jnp.where(tril, a, 0.0)
        o = o + jnp.dot(a.astype(BF16), vh, preferred_element_type=F32)
        st_ref[h] = s_t * jnp.exp(g_last) + lax.dot_general(vh, k_dec, tn, preferred_element_type=F32)
        on = o * lax.rsqrt(jnp.mean(o * o, axis=-1, keepdims=True) + RMS_EPS) * gn_ref[...]
        o_ref[:, vs] = (on * _silu(r_ref[:, vs].astype(F32))).astype(o_ref.dtype)

    @pl.when(n == nc - 1)
    def _():
        for h in range(H_C):
            so_ref[0, h] = st_ref[h].T


def _gla(o_cat, zq, g, s0, s0_b0, gnorm, row_blk0, n_seq, n_chunks, *, chunk):
    qc, kc, vc, rc = RS_QC // QK_C, RS_KC // QK_C, RS_VC // W_C, RS_RC // W_C
    o_col = OC_C // W_C
    row = lambda b, n: row_blk0 + b * n_chunks + n
    body, lead_specs, lead_args, aliases = _carried(o_cat, functools.partial(_gla_kernel, chunk=chunk))
    return pl.pallas_call(
        body,
        grid=(n_seq, n_chunks),
        in_specs=lead_specs + [
            pl.BlockSpec((chunk, QK_C), lambda b, n: (row(b, n), qc)),
            pl.BlockSpec((chunk, QK_C), lambda b, n: (row(b, n), kc)),
            pl.BlockSpec((chunk, W_C), lambda b, n: (row(b, n), vc)),
            pl.BlockSpec((chunk, W_C), lambda b, n: (row(b, n), rc)),
            pl.BlockSpec((chunk, QK_C), lambda b, n: (row(b, n), 0)),
            pl.BlockSpec((1, H_C, DK_C, DV_C), lambda b, n: (s0_b0 + b, 0, 0, 0)),
            pl.BlockSpec((1, DV_C), lambda b, n: (0, 0))],
        out_specs=[pl.BlockSpec((chunk, W_C), lambda b, n: (row(b, n), o_col)),
                   pl.BlockSpec((1, H_C, DK_C, DV_C), lambda b, n: (b, 0, 0, 0))],
        out_shape=[jax.ShapeDtypeStruct((zq.shape[0], MIX_WIDTH), BF16),
                   jax.ShapeDtypeStruct((n_seq, H_C, DK_C, DV_C), F32)],
        input_output_aliases=aliases,
        scratch_shapes=[pltpu.VMEM((H_C, DV_C, DK_C), F32)],
        compiler_params=_cparams(("arbitrary", "arbitrary")),
        name="gla",
    )(*lead_args, zq, zq, zq, zq, g, s0, gnorm.reshape(1, DV_C))


def _router_kernel(x_ref, w_ref, o_ref, *, n_experts):
    logits = jnp.dot(x_ref[...], w_ref[...], preferred_element_type=F32, precision=lax.Precision.HIGHEST)
    lane = lax.broadcasted_iota(jnp.int32, logits.shape, 1)
    lg = jnp.where(lane < n_experts, logits, NEG_INF)
    m0 = jnp.max(lg, axis=1, keepdims=True)
    i0 = jnp.min(jnp.where(lg == m0, lane, LANE), axis=1, keepdims=True)
    lg1 = jnp.where(lane == i0, NEG_INF, lg)
    m1 = jnp.max(lg1, axis=1, keepdims=True)
    i1 = jnp.min(jnp.where(lg1 == m1, lane, LANE), axis=1, keepdims=True)
    e1 = jnp.exp(m1 - m0)
    den = 1.0 + e1
    out = jnp.where(lane == 0, i0.astype(F32),
                    jnp.where(lane == 1, i1.astype(F32),
                              jnp.where(lane == 2, 1.0 / den, jnp.where(lane == 3, e1 / den, 0.0))))
    o_ref[...] = out


def _router(x_f32, w_router_pad, n_experts, *, tm):
    n, d = x_f32.shape
    return pl.pallas_call(
        functools.partial(_router_kernel, n_experts=n_experts),
        grid=(n // tm,),
        in_specs=[pl.BlockSpec((tm, d), lambda t: (t, 0)),
                  pl.BlockSpec((d, LANE), lambda t: (0, 0))],
        out_specs=pl.BlockSpec((tm, LANE), lambda t: (t, 0)),
        out_shape=jax.ShapeDtypeStruct((n, LANE), F32),
        compiler_params=_cparams(("arbitrary",)),
        name="moe_router",
    )(x_f32, w_router_pad)


def _gather_kernel(src_ref, x_hbm, o_ref, buf_ref, sem, *, tr):
    def row_copy(r):
        return pltpu.make_async_copy(x_hbm.at[pl.ds(src_ref[r], 1)], buf_ref.at[pl.ds(r, 1)], sem)

    def issue(r, c):
        row_copy(r).start()
        return c

    def drain(r, c):
        row_copy(r).wait()
        return c

    lax.fori_loop(0, tr, issue, 0)
    lax.fori_loop(0, tr, drain, 0)
    o_ref[...] = buf_ref[...].astype(o_ref.dtype)


def _gather_rows(x_f32, src, *, tr):
    d = x_f32.shape[1]
    rows = src.shape[0]
    return pl.pallas_call(
        functools.partial(_gather_kernel, tr=tr),
        grid=(rows // tr,),
        in_specs=[pl.BlockSpec((tr,), lambda t: (t,), memory_space=pltpu.SMEM),
                  pl.BlockSpec(memory_space=pl.ANY)],
        out_specs=pl.BlockSpec((tr, d), lambda t: (t, 0)),
        out_shape=jax.ShapeDtypeStruct((rows, d), BF16),
        scratch_shapes=[pltpu.VMEM((tr, d), F32), pltpu.SemaphoreType.DMA(())],
        compiler_params=_cparams(("arbitrary",)),
        name="moe_gather",
    )(src, x_f32)


def _combine_kernel(s0_ref, s1_ref, y_hbm, x_ref, g_ref, b_ref, of_ref, ob_ref, buf0, buf1, sem, *, tc, alpha):
    def copies(r):
        return (pltpu.make_async_copy(y_hbm.at[pl.ds(s0_ref[r], 1)], buf0.at[pl.ds(r, 1)], sem),
                pltpu.make_async_copy(y_hbm.at[pl.ds(s1_ref[r], 1)], buf1.at[pl.ds(r, 1)], sem))

    def issue(r, c):
        a, b = copies(r)
        a.start()
        b.start()
        return c

    def drain(r, c):
        a, b = copies(r)
        a.wait()
        b.wait()
        return c

    lax.fori_loop(0, tc, issue, 0)
    lax.fori_loop(0, tc, drain, 0)
    y = _layer_norm(alpha * x_ref[...] + (buf0[...] + buf1[...]), g_ref[...], b_ref[...])
    of_ref[...] = y
    ob_ref[...] = y.astype(BF16)


def _moe_combine(ys, slot0, slot1, x_f32, ln_g, ln_b, alpha, *, tc):
    n, d = x_f32.shape
    return pl.pallas_call(
        functools.partial(_combine_kernel, tc=tc, alpha=alpha),
        grid=(n // tc,),
        in_specs=[pl.BlockSpec((tc,), lambda t: (t,), memory_space=pltpu.SMEM),
                  pl.BlockSpec((tc,), lambda t: (t,), memory_space=pltpu.SMEM),
                  pl.BlockSpec(memory_space=pl.ANY),
                  pl.BlockSpec((tc, d), lambda t: (t, 0)),
                  pl.BlockSpec((1, d), lambda t: (0, 0)),
                  pl.BlockSpec((1, d), lambda t: (0, 0))],
        out_specs=[pl.BlockSpec((tc, d), lambda t: (t, 0)), pl.BlockSpec((tc, d), lambda t: (t, 0))],
        out_shape=[jax.ShapeDtypeStruct((n, d), F32), jax.ShapeDtypeStruct((n, d), BF16)],
        scratch_shapes=[pltpu.VMEM((tc, d), F32), pltpu.VMEM((tc, d), F32), pltpu.SemaphoreType.DMA(())],
        compiler_params=_cparams(("arbitrary",)),
        name="moe_combine",
    )(slot0, slot1, ys, x_f32, ln_g.reshape(1, d), ln_b.reshape(1, d))


def _moe_plan(route, n_experts, tm):
    n = route.shape[0]
    e = jnp.concatenate([route[:, 0], route[:, 1]]).astype(jnp.int32)
    gate = jnp.concatenate([route[:, 2], route[:, 3]])
    token = jnp.concatenate([jnp.arange(n, dtype=jnp.int32)] * TOP_K)
    onehot = (e[:, None] == jnp.arange(n_experts, dtype=jnp.int32)[None, :]).astype(jnp.int32)
    csum = jnp.cumsum(onehot, axis=0)
    rank = jnp.take_along_axis(csum, e[:, None], axis=1)[:, 0] - 1
    counts = csum[-1]
    tiles_per = (counts + tm - 1) // tm
    tile_end = jnp.cumsum(tiles_per)
    row_start = (tile_end - tiles_per) * tm
    dest = row_start[e] + rank
    n_tiles = (TOP_K * n + n_experts * (tm - 1)) // tm
    rows = n_tiles * tm
    src = jnp.zeros((rows,), jnp.int32).at[dest].set(token)
    gate_rows = jnp.zeros((rows,), F32).at[dest].set(gate)
    tile_expert = jnp.minimum(
        jnp.searchsorted(tile_end, jnp.arange(n_tiles, dtype=jnp.int32), side="right"), n_experts - 1).astype(jnp.int32)
    n_used = tile_end[-1:].astype(jnp.int32)
    return src, gate_rows.reshape(rows, 1), tile_expert, n_used, dest[:n], dest[n:]


def _ffn_dense(x_f32, x_bf, w1, w3, w2, ln_g, ln_b, alpha, cfg):
    n, d = x_bf.shape
    f = w1.shape[-1]
    tm, tf = cfg["ffn_tm"], cfg["ffn_tf"]
    te = jnp.zeros((n // tm,), jnp.int32)
    nu = jnp.full((1,), n // tm, jnp.int32)
    a, = _matmul_wres(x_bf, [w1[None], w3[None]], te, nu, tm=tm, tn=tf, col0=0, n_cols=f,
                      out_dtypes=[BF16], epilogue=_epi_swiglu, name="ffn_gate_up")
    return _proj_resid_ln(a, w2, x_f32, ln_g, ln_b, alpha, tm=cfg["ln_tm"], tk=cfg["down_tk"], name="ffn_down_ln")


def _ffn_moe(x_f32, x_bf, w_router_pad, n_experts, w1, w3, w2, ln_g, ln_b, alpha, cfg):
    n, d = x_bf.shape
    f = w1.shape[-1]
    tm, tf = cfg["moe_tm"], cfg["ffn_tf"]
    route = _router(x_f32, w_router_pad, n_experts, tm=cfg["row_tm"])
    src, gate_rows, tile_expert, n_used, slot0, slot1 = _moe_plan(route, n_experts, tm)
    xs = _gather_rows(x_f32, src, tr=cfg["gather_tr"])
    a, = _matmul_wres(xs, [w1, w3], tile_expert, n_used, tm=tm, tn=tf, col0=0, n_cols=f,
                      out_dtypes=[BF16], epilogue=_epi_swiglu, name="moe_gate_up")
    ys, = _matmul_acc(a, w2, tile_expert, n_used, [gate_rows],
                      [pl.BlockSpec((tm, 1), lambda t, k, te, nu: (t, 0))],
                      tm=tm, tk=cfg["moe_down_tk"], out_dtypes=[F32], epilogue=_epi_row_scale, name="moe_down")
    return _moe_combine(ys, slot0, slot1, x_f32, ln_g, ln_b, alpha, tc=cfg["combine_tc"])


def _split_w_in(w_in):
    sizes = (W_A, W_A, W_A, H_A, W_B, W_B, W_B, QK_C, QK_C, W_C, W_C, GATE_RANK)
    offs = np.concatenate([[0], np.cumsum(sizes)])
    w_t = w_in.T
    part = lambda i: w_t[offs[i]:offs[i + 1]]
    qa, ka, va, fa, qb, kb, vb, qc, kc, vc, rc, ac = [part(i) for i in range(12)]
    main = jnp.concatenate([ka, va, kb, vb, qc, kc, vc, rc, qa, qb], axis=0).astype(BF16)
    d = w_in.shape[0]
    fa_t = jnp.concatenate([fa, jnp.zeros((16 - H_A, d), F32)], axis=0).astype(BF16)
    ac_t = jnp.concatenate([ac, jnp.zeros((LANE - GATE_RANK, d), F32)], axis=0).astype(BF16)
    return main, fa_t, ac_t


def _mixer(x_bf, dims, layer, caches, params, cfg):
    batch, seq, dec_batch, t_new = dims
    fk_all, fv_all, bk_all, bv_all, gla_all, c_flogf, c_bk, c_bv = caches
    w_main, w_fa_t, w_ac_pad, b_forget, rel_bias, w_gate_up, b_gate, gla_norm_g = params
    n = x_bf.shape[0]
    n_p = batch * seq
    past = fk_all.shape[1]
    n_band = bk_all.shape[1]
    assert seq % CHUNK == 0 and t_new == CHUNK and past % CHUNK == 0 and n_band % LANE == 0
    assert n_band == min(BAND_PAST, past) and n_p % t_new == 0

    tm, tn = cfg["proj_tm"], cfg["proj_tn"]
    te = jnp.zeros((n // tm,), jnp.int32)
    nu = jnp.full((1,), n // tm, jnp.int32)
    zs_f32, zs = _matmul_wres(x_bf, [w_main[None]], te, nu, tm=tm, tn=tn, col0=0, n_cols=W_STATE,
                              out_dtypes=[F32, BF16], epilogue=_epi_store, name="proj_state", w_transposed=True)
    zr, = _matmul_wres(x_bf, [w_main[None]], te, nu, tm=tm, tn=tn, col0=W_STATE, n_cols=W_REST,
                       out_dtypes=[BF16], epilogue=_epi_store, name="proj_rest", w_transposed=True)

    b_col = jnp.zeros((16, 1), F32).at[:H_A, 0].set(b_forget)
    logf_t = _forget_logits(x_bf, w_fa_t, b_col, tm=cfg["row_tm"])
    w_up_pad = jnp.zeros((LANE, QK_C), F32).at[:GATE_RANK].set(w_gate_up).astype(BF16)
    g = _gla_gates(x_bf, w_ac_pad, w_up_pad, b_gate, tm=cfg["row_tm"])

    c_p = _cumsum_last(logf_t[:, :n_p].reshape(16 * batch, seq)).reshape(16 * batch, 1, seq)
    o_cat = jnp.zeros((n, MIX_WIDTH), BF16)
    o_cat = _fox_prompt(o_cat, zr, zs, c_p, batch, seq, tq=cfg["fox_tq"])
    logf_s = logf_t[:H_A, n_p:].reshape(H_A, dec_batch, t_new).transpose(1, 0, 2)
    logf_all = jnp.concatenate([c_flogf.transpose(0, 2, 1), logf_s], axis=2)
    c_s = _cumsum_last(logf_all.reshape(dec_batch * H_A, past + t_new))
    hg = cfg["fox_hg"]
    c_s = c_s.reshape(dec_batch * (H_A // hg), hg, past + t_new)
    cache_b0 = layer * dec_batch
    o_cat = _fox_sample(o_cat, zr, zs, fk_all, fv_all, cache_b0, c_s, n_p // t_new, dec_batch, t_new, hg=hg)

    bias_p, bias_s = _band_bias(rel_bias, tq=cfg["band_tq"], t_new=t_new, n_past=n_band)
    o_cat = _band_prompt(o_cat, zr, zs, bias_p, batch, seq, tq=cfg["band_tq"])
    o_cat = _band_sample(o_cat, zr, zs, bk_all, bv_all, cache_b0, bias_s, n_p // t_new, dec_batch, t_new)

    s_zero = jnp.zeros((batch, H_C, DK_C, DV_C), F32)
    o_cat, s_p = _gla(o_cat, zr, g, s_zero, 0, gla_norm_g, 0, batch, seq // CHUNK, chunk=CHUNK)
    o_cat, s_s = _gla(o_cat, zr, g, gla_all, cache_b0, gla_norm_g, n_p // CHUNK, dec_batch, 1, chunk=CHUNK)

    ka, va = zs_f32[:, ST_KA:ST_KA + W_A], zs_f32[:, ST_VA:ST_VA + W_A]
    kb, vb = zs_f32[:, ST_KB:ST_KB + W_B], zs_f32[:, ST_VB:ST_VB + W_B]
    heads = lambda t, b, l, h: t.reshape(b, l, h, HEAD_DIM)
    nbp = min(BAND_PAST, seq)
    logf_p = logf_t[:H_A, :n_p].reshape(H_A, batch, seq).transpose(1, 2, 0)
    st_p = (heads(ka[:n_p], batch, seq, H_A), heads(va[:n_p], batch, seq, H_A), logf_p,
            heads(kb[:n_p], batch, seq, H_B)[:, seq - nbp:], heads(vb[:n_p], batch, seq, H_B)[:, seq - nbp:], s_p)
    kb_s, vb_s = heads(kb[n_p:], dec_batch, t_new, H_B), heads(vb[n_p:], dec_batch, t_new, H_B)
    st_s = (heads(ka[n_p:], dec_batch, t_new, H_A), heads(va[n_p:], dec_batch, t_new, H_A),
            logf_s.transpose(0, 2, 1),
            jnp.concatenate([c_bk, kb_s], axis=1)[:, t_new:], jnp.concatenate([c_bv, vb_s], axis=1)[:, t_new:], s_s)
    return o_cat, st_p, st_s


def _config(n, d, f, seq):
    return dict(
        proj_tm=_tile(n, 1024, 256), proj_tn=_tile(W_STATE, 1024, 512),
        row_tm=_tile(n, 512, 256),
        ffn_tm=_tile(n, 1024, 256), ffn_tf=_tile(f, 512, 128),
        ln_tm=_tile(n, 512, 128), down_tk=_tile(f, 512, 128), out_tk=_tile(MIX_WIDTH, 512, 128),
        moe_down_tk=_tile(f, 1024, 128),
        moe_tm=512 if n >= 4096 else 128, gather_tr=256 if n >= 4096 else 128,
        combine_tc=_tile(n, 256, 128),
        fox_tq=_tile(seq, 512, 128), fox_hg=4, band_tq=_tile(seq, 512, 128),
    )


def kernel(x_prompt, x_sample, cache_fox_k, cache_fox_v, cache_fox_logf, cache_band_k, cache_band_v, state_gla,
           w_in, b_forget, rel_bias, w_gate_up, b_gate, gla_norm_g, w_out,
           ln1_g, ln1_b, ln2_g, ln2_b, ffn_w1, ffn_w3, ffn_w2,
           moe_router, moe_w1, moe_w3, moe_w2):
    batch, seq, d = x_prompt.shape
    dec_batch, t_new, _ = x_sample.shape
    depth = w_in.shape[0]
    f = ffn_w1.shape[-1]
    n_experts = moe_router.shape[-1]
    n_p, n_s = batch * seq, dec_batch * t_new
    n = n_p + n_s
    alpha = (2.0 * depth) ** 0.25
    cfg = _config(n, d, f, seq)
    dims = (batch, seq, dec_batch, t_new)

    x = jnp.concatenate([x_prompt.reshape(n_p, d), x_sample.reshape(n_s, d)], axis=0)
    x_bf = x.astype(BF16)
    sp_all, ss_all = [], []
    past, n_band = cache_fox_k.shape[2], cache_band_k.shape[2]
    head_major = lambda c: c.transpose(0, 1, 3, 2, 4).reshape(depth * dec_batch * H_A, past, HEAD_DIM)
    fk_all, fv_all = head_major(cache_fox_k), head_major(cache_fox_v)
    bk_all = cache_band_k.reshape(depth * dec_batch, n_band, W_B)
    bv_all = cache_band_v.reshape(depth * dec_batch, n_band, W_B)
    gla_all = state_gla.reshape(depth * dec_batch, H_C, DK_C, DV_C)
    for i in range(depth):
        w_main, w_fa_t, w_ac_pad = _split_w_in(w_in[i])
        caches = (fk_all, fv_all, bk_all, bv_all, gla_all, cache_fox_logf[i], cache_band_k[i], cache_band_v[i])
        params = (w_main, w_fa_t, w_ac_pad, b_forget[i], rel_bias[i], w_gate_up[i], b_gate[i], gla_norm_g[i])
        o_cat, st_p, st_s = _mixer(x_bf, dims, i, caches, params, cfg)
        sp_all.append(st_p)
        ss_all.append(st_s)
        w_o = w_out[i]
        w_o = jnp.concatenate([w_o[W_A + W_B:], w_o[:W_A], w_o[W_A:W_A + W_B]], axis=0).astype(BF16)
        x, x_bf = _proj_resid_ln(o_cat, w_o, x, ln1_g[i], ln1_b[i], alpha,
                                 tm=cfg["ln_tm"], tk=cfg["out_tk"], name="out_proj_ln")
        j = i // 2
        if i % 2 == 0:
            x, x_bf = _ffn_dense(x, x_bf, ffn_w1[j].astype(BF16), ffn_w3[j].astype(BF16), ffn_w2[j].astype(BF16),
                                 ln2_g[i], ln2_b[i], alpha, cfg)
        else:
            w_router_pad = jnp.zeros((d, LANE), F32).at[:, :n_experts].set(moe_router[j])
            x, x_bf = _ffn_moe(x, x_bf, w_router_pad, n_experts, moe_w1[j].astype(BF16), moe_w3[j].astype(BF16),
                               moe_w2[j].astype(BF16), ln2_g[i], ln2_b[i], alpha, cfg)
    stack = lambda states, k: jnp.stack([s[k] for s in states], axis=0)
    y_p = x[:n_p].reshape(batch, seq, d)
    y_s = x[n_p:].reshape(dec_batch, t_new, d)
    return (y_p, y_s) + tuple(stack(sp_all, k) for k in range(6)) + tuple(stack(ss_all, k) for k in range(6))
```

```python
import functools
import math

import jax
import jax.numpy as jnp
import numpy as np
from jax import lax
from jax.experimental import pallas as pl
from jax.experimental.pallas import tpu as pltpu

CHUNK = 64
HEAD_DIM = 128
H_A = 12
H_B = 8
H_C = 6
DK_C = 128
DV_C = 256
W_A = H_A * HEAD_DIM
W_B = H_B * HEAD_DIM
QK_C = H_C * DK_C
W_C = H_C * DV_C
GATE_RANK = 16
GLA_TAU = 16.0
BAND_CHUNKS = 8
BAND_PAST = BAND_CHUNKS * CHUNK
MAX_REL = 128
TOP_K = 2
LN_EPS = 1e-5
RMS_EPS = 1e-6
ATTN_SCALE = HEAD_DIM ** -0.5

LANE = 128
NEG_INF = float("-inf")
LOG2E = math.log2(math.e)
F32 = jnp.float32
BF16 = jnp.bfloat16
VMEM_LIMIT = 56 * 1024 * 1024

ST_KA, ST_VA, ST_KB, ST_VB = 0, W_A, 2 * W_A, 2 * W_A + W_B
W_STATE = 2 * W_A + 2 * W_B
RS_QC, RS_KC, RS_VC, RS_RC, RS_QA, RS_QB = 0, QK_C, 2 * QK_C, 2 * QK_C + W_C, 2 * QK_C + 2 * W_C, 2 * QK_C + 2 * W_C + W_A
W_REST = RS_QB + W_B
OC_C, OC_A, OC_B = 0, W_C, W_C + W_A
MIX_WIDTH = W_A + W_B + W_C


def _tile(dim, pref, mult):
    if dim <= pref:
        return dim
    t = (pref // mult) * mult
    while t > mult and dim % t:
        t -= mult
    assert dim % t == 0, (dim, pref, mult)
    return t


def _cparams(sem):
    return pltpu.CompilerParams(dimension_semantics=sem, vmem_limit_bytes=VMEM_LIMIT)


def _carry_output(body):
    def wrapped(_, *refs):
        body(*refs)
    return wrapped


def _carried(o_cat, body):
    return _carry_output(body), [pl.BlockSpec(memory_space=pl.ANY)], [o_cat], {0: 0}


def _log_sigmoid(x):
    return jnp.minimum(x, 0.0) - jnp.log(1.0 + jnp.exp(-jnp.abs(x)))


def _silu(x):
    return x / (1.0 + jnp.exp(-x))


def _layer_norm(x, g, b):
    mu = jnp.mean(x, axis=-1, keepdims=True)
    xc = x - mu
    var = jnp.mean(xc * xc, axis=-1, keepdims=True)
    return xc * lax.rsqrt(var + LN_EPS) * g + b


NT_DIMS = (((1,), (1,)), ((), ()))


def _wres_kernel(te_ref, nu_ref, x_ref, *refs, n_w, n_out, epilogue, w_transposed, cast_blocks):
    w_refs = refs[:n_w]
    out_refs = refs[n_w + (cast_blocks > 0):][:n_out]
    t = pl.program_id(1)

    @pl.when(t < nu_ref[0])
    def _():
        x = x_ref[...]
        if w_transposed:
            accs = [lax.dot_general(x, w[0], NT_DIMS, preferred_element_type=F32) for w in w_refs]
        else:
            accs = [jnp.dot(x, w[0], preferred_element_type=F32) for w in w_refs]
        epilogue(accs, out_refs)

    @pl.when(t >= nu_ref[0])
    def _():
        for o in out_refs:
            o[...] = jnp.zeros(o.shape, o.dtype)

    if cast_blocks:
        src_ref, dst_ref = refs[n_w], refs[-1]
        step = pl.program_id(0) * pl.num_programs(1) + t

        @pl.when(step < cast_blocks)
        def _():
            dst_ref[...] = src_ref[...].astype(dst_ref.dtype)


def _matmul_wres(x, ws, tile_expert, n_used, *, tm, tn, col0, n_cols, out_dtypes, epilogue, name,
                 w_transposed=False, cast_src=None, cast_rows=0):
    rows, kdim = x.shape
    nt = rows // tm
    nj = n_cols // tn
    j0 = col0 // tn
    assert rows % tm == 0 and n_cols % tn == 0 and col0 % tn == 0
    if w_transposed:
        w_spec = pl.BlockSpec((1, tn, kdim), lambda j, t, te, nu: (te[t], j + j0, 0))
    else:
        w_spec = pl.BlockSpec((1, kdim, tn), lambda j, t, te, nu: (te[t], 0, j + j0))
    in_specs = [pl.BlockSpec((tm, kdim), lambda j, t, te, nu: (t, 0))] + [w_spec for _ in ws]
    out_specs = [pl.BlockSpec((tm, tn), lambda j, t, te, nu: (t, j)) for _ in out_dtypes]
    out_shape = [jax.ShapeDtypeStruct((rows, n_cols), dt) for dt in out_dtypes]
    args = [x, *ws]
    cast_blocks = 0
    if cast_src is not None:
        cast_blocks = cast_src.shape[0] // cast_rows
        assert cast_src.shape[0] % cast_rows == 0 and cast_blocks <= nj * nt
        cast_spec = pl.BlockSpec((cast_rows, cast_src.shape[1]),
                                 lambda j, t, te, nu: (jnp.minimum(j * nt + t, cast_blocks - 1), 0))
        in_specs.append(cast_spec)
        out_specs.append(cast_spec)
        out_shape.append(jax.ShapeDtypeStruct(cast_src.shape, BF16))
        args.append(cast_src)
    grid_spec = pltpu.PrefetchScalarGridSpec(
        num_scalar_prefetch=2, grid=(nj, nt), in_specs=in_specs, out_specs=out_specs)
    return pl.pallas_call(
        functools.partial(_wres_kernel, n_w=len(ws), n_out=len(out_dtypes), epilogue=epilogue,
                          w_transposed=w_transposed, cast_blocks=cast_blocks),
        grid_spec=grid_spec,
        out_shape=out_shape,
        compiler_params=_cparams(("arbitrary", "arbitrary")),
        name=name,
    )(tile_expert, n_used, *args)


def _epi_store(accs, outs):
    for o in outs:
        o[...] = accs[0].astype(o.dtype)


def _epi_swiglu(accs, outs):
    outs[0][...] = (_silu(accs[0]) * accs[1]).astype(outs[0].dtype)


ROW_SLAB = 32


def _acc_kernel(te_ref, nu_ref, a_ref, w_ref, *refs, n_extra, epilogue):
    extra = refs[:n_extra]
    outs = refs[n_extra:]
    acc_ref = outs[0]
    t = pl.program_id(0)
    k = pl.program_id(1)
    nk = pl.num_programs(1)
    used = t < nu_ref[0]

    @pl.when(used & (k == 0))
    def _():
        acc_ref[...] = jnp.dot(a_ref[...], w_ref[0], preferred_element_type=F32)

    @pl.when(used & (k > 0))
    def _():
        acc_ref[...] += jnp.dot(a_ref[...], w_ref[0], preferred_element_type=F32)

    @pl.when(used & (k == nk - 1))
    def _():
        def slab(i, c):
            rows = pl.ds(pl.multiple_of(i * ROW_SLAB, ROW_SLAB), ROW_SLAB)
            epilogue(rows, extra, outs)
            return c

        lax.fori_loop(0, acc_ref.shape[0] // ROW_SLAB, slab, 0)

    @pl.when(jnp.logical_not(used) & (k == nk - 1))
    def _():
        for o in outs:
            o[...] = jnp.zeros(o.shape, o.dtype)


def _matmul_acc(a, w, tile_expert, n_used, extra, extra_specs, *, tm, tk, out_dtypes, epilogue, name):
    rows, kdim = a.shape
    n_out = w.shape[2]
    nt = rows // tm
    nk = kdim // tk
    assert rows % tm == 0 and kdim % tk == 0 and tm % ROW_SLAB == 0 and out_dtypes[0] == F32
    grid_spec = pltpu.PrefetchScalarGridSpec(
        num_scalar_prefetch=2,
        grid=(nt, nk),
        in_specs=[pl.BlockSpec((tm, tk), lambda t, k, te, nu: (t, k)),
                  pl.BlockSpec((1, tk, n_out), lambda t, k, te, nu: (te[t], k, 0))] + extra_specs,
        out_specs=[pl.BlockSpec((tm, n_out), lambda t, k, te, nu: (t, 0)) for _ in out_dtypes],
    )
    return pl.pallas_call(
        functools.partial(_acc_kernel, n_extra=len(extra), epilogue=epilogue),
        grid_spec=grid_spec,
        out_shape=[jax.ShapeDtypeStruct((rows, n_out), dt) for dt in out_dtypes],
        compiler_params=_cparams(("arbitrary", "arbitrary")),
        name=name,
    )(tile_expert, n_used, a, w, *extra)


def _epi_resid_ln(rows, extra, outs, *, alpha):
    x_ref, g_ref, b_ref = extra
    y = _layer_norm(alpha * x_ref[rows, :] + outs[0][rows, :], g_ref[...], b_ref[...])
    outs[0][rows, :] = y
    outs[1][rows, :] = y.astype(BF16)


def _epi_row_scale(rows, extra, outs):
    outs[0][rows, :] = outs[0][rows, :] * extra[0][rows, :]


def _proj_resid_ln(a, w, x_f32, ln_g, ln_b, alpha, *, tm, tk, name):
    rows = a.shape[0]
    d = w.shape[-1]
    te = jnp.zeros((rows // tm,), jnp.int32)
    nu = jnp.full((1,), rows // tm, jnp.int32)
    extra_specs = [pl.BlockSpec((tm, d), lambda t, k, te, nu: (t, 0)),
                   pl.BlockSpec((1, d), lambda t, k, te, nu: (0, 0)),
                   pl.BlockSpec((1, d), lambda t, k, te, nu: (0, 0))]
    return _matmul_acc(a, w.reshape((1,) + w.shape[-2:]), te, nu,
                       [x_f32, ln_g.reshape(1, d), ln_b.reshape(1, d)], extra_specs,
                       tm=tm, tk=tk, out_dtypes=[F32, BF16],
                       epilogue=functools.partial(_epi_resid_ln, alpha=alpha), name=name)


def _forget_kernel(wt_ref, b_ref, x_ref, o_ref):
    fa = lax.dot_general(wt_ref[...], x_ref[...], (((1,), (1,)), ((), ())), preferred_element_type=F32)
    o_ref[...] = _log_sigmoid(fa + b_ref[...])


def _forget_logits(x_bf, w_fa_t, b_col, *, tm):
    n, d = x_bf.shape
    hp = w_fa_t.shape[0]
    return pl.pallas_call(
        _forget_kernel,
        grid=(n // tm,),
        in_specs=[pl.BlockSpec((hp, d), lambda t: (0, 0)),
                  pl.BlockSpec((hp, 1), lambda t: (0, 0)),
                  pl.BlockSpec((tm, d), lambda t: (t, 0))],
        out_specs=pl.BlockSpec((hp, tm), lambda t: (0, t)),
        out_shape=jax.ShapeDtypeStruct((hp, n), F32),
        compiler_params=_cparams(("arbitrary",)),
        name="forget_logits",
    )(w_fa_t, b_col, x_bf)


def _gate_kernel(x_ref, wa_ref, wu_ref, b_ref, o_ref):
    ac = lax.dot_general(x_ref[...], wa_ref[...], NT_DIMS, preferred_element_type=F32)
    pre = jnp.dot(ac.astype(BF16), wu_ref[...], preferred_element_type=F32) + b_ref[...]
    o_ref[...] = _log_sigmoid(pre) * (1.0 / GLA_TAU)


def _gla_gates(x_bf, w_ac_pad, w_up_pad, b_gate, *, tm):
    n, d = x_bf.shape
    return pl.pallas_call(
        _gate_kernel,
        grid=(n // tm,),
        in_specs=[pl.BlockSpec((tm, d), lambda t: (t, 0)),
                  pl.BlockSpec((LANE, d), lambda t: (0, 0)),
                  pl.BlockSpec((LANE, QK_C), lambda t: (0, 0)),
                  pl.BlockSpec((1, QK_C), lambda t: (0, 0))],
        out_specs=pl.BlockSpec((tm, QK_C), lambda t: (t, 0)),
        out_shape=jax.ShapeDtypeStruct((n, QK_C), F32),
        compiler_params=_cparams(("arbitrary",)),
        name="gla_gates",
    )(x_bf, w_ac_pad, w_up_pad, b_gate.reshape(1, QK_C))


def _cumsum_kernel(x_ref, o_ref, *, t_len):
    rows = x_ref.shape[0]
    ii = lax.broadcasted_iota(jnp.int32, (LANE, LANE), 0)
    jj = lax.broadcasted_iota(jnp.int32, (LANE, LANE), 1)
    upper = (ii <= jj).astype(F32)
    carry = jnp.zeros((rows, 1), F32)
    for c0 in range(0, t_len, LANE):
        w = min(LANE, t_len - c0)
        blk = x_ref[:, c0:c0 + w]
        cs = jnp.dot(blk, upper[:w, :w], preferred_element_type=F32, precision=lax.Precision.HIGHEST) + carry
        o_ref[:, c0:c0 + w] = cs
        carry = cs[:, w - 1:w]


def _cumsum_last(x):
    rows, t_len = x.shape
    rb = 8
    assert rows % rb == 0
    return pl.pallas_call(
        functools.partial(_cumsum_kernel, t_len=t_len),
        grid=(rows // rb,),
        in_specs=[pl.BlockSpec((rb, t_len), lambda r: (r, 0))],
        out_specs=pl.BlockSpec((rb, t_len), lambda r: (r, 0)),
        out_shape=jax.ShapeDtypeStruct((rows, t_len), F32),
        compiler_params=_cparams(("arbitrary",)),
        name="cumsum_time",
    )(x)


def _fox_prompt_kernel(q_ref, k_ref, v_ref, c_ref, o_ref, *, tq, hp):
    qi = pl.program_id(2)
    q0 = pl.multiple_of(qi * tq, tq)
    heads = [slice(h * HEAD_DIM, (h + 1) * HEAD_DIM) for h in range(hp)]
    qs = [q_ref[:, sl] for sl in heads]
    c_first = [c_ref[0, h:h + 1, pl.ds(q0, LANE)][:, :1] for h in range(hp)]

    def step(k0, carry, masked):
        out = []
        for h, sl in enumerate(heads):
            m, l, acc = carry[h]
            kb = k_ref[pl.ds(k0, tq), sl]
            vb = v_ref[pl.ds(k0, tq), sl]
            s = lax.dot_general(qs[h], kb, NT_DIMS, preferred_element_type=F32) * (ATTN_SCALE * LOG2E)
            s = s + (c_first[h] - c_ref[0, h:h + 1, pl.ds(k0, tq)]) * LOG2E
            if masked:
                ii = lax.broadcasted_iota(jnp.int32, (tq, tq), 0)
                jj = lax.broadcasted_iota(jnp.int32, (tq, tq), 1)
                s = jnp.where(jj <= ii, s, NEG_INF)
            m_new = jnp.maximum(m, jnp.max(s, axis=1, keepdims=True))
            a = jnp.exp2(m - m_new)
            p = jnp.exp2(s - m_new)
            l = a * l + jnp.sum(p, axis=1, keepdims=True)
            acc = a * acc + jnp.dot(p.astype(BF16), vb, preferred_element_type=F32)
            out.append((m_new, l, acc))
        return tuple(out)

    init = tuple((jnp.full((tq, 1), NEG_INF, F32), jnp.zeros((tq, 1), F32), jnp.zeros((tq, HEAD_DIM), F32))
                 for _ in heads)
    carry = lax.fori_loop(0, qi, lambda kj, c: step(pl.multiple_of(kj * tq, tq), c, False), init)
    carry = step(q0, carry, True)
    for (m, l, acc), sl in zip(carry, heads):
        o_ref[:, sl] = (acc / l).astype(o_ref.dtype)


def _fox_prompt(o_cat, zq, zkv, c_rows, batch, seq, *, tq, hp):
    nq = seq // tq
    wg = hp * HEAD_DIM
    ng = H_A // hp
    q_col, k_col, v_col, o_col = RS_QA // wg, ST_KA // wg, ST_VA // wg, OC_A // wg
    body, lead_specs, lead_args, aliases = _carried(o_cat, functools.partial(_fox_prompt_kernel, tq=tq, hp=hp))
    return pl.pallas_call(
        body,
        grid=(batch, ng, nq),
        in_specs=lead_specs + [
            pl.BlockSpec((tq, wg), lambda b, g, i: (b * nq + i, q_col + g)),
            pl.BlockSpec((seq, wg), lambda b, g, i: (b, k_col + g)),
            pl.BlockSpec((seq, wg), lambda b, g, i: (b, v_col + g)),
            pl.BlockSpec((1, hp, seq), lambda b, g, i: (b * ng + g, 0, 0))],
        out_specs=pl.BlockSpec((tq, wg), lambda b, g, i: (b * nq + i, o_col + g)),
        out_shape=jax.ShapeDtypeStruct((zq.shape[0], MIX_WIDTH), BF16),
        input_output_aliases=aliases,
        compiler_params=_cparams(("arbitrary", "arbitrary", "arbitrary")),
        name="fox_prompt",
    )(*lead_args, zq, zkv, zkv, c_rows)


def _fox_sample_kernel(q_ref, kn_ref, vn_ref, ck_ref, cv_ref, c_ref, o_ref, *, hg, past, t_new):
    ii = lax.broadcasted_iota(jnp.int32, (t_new, t_new), 0)
    jj = lax.broadcasted_iota(jnp.int32, (t_new, t_new), 1)
    causal = jj <= ii
    for hh in range(hg):
        sl = slice(hh * HEAD_DIM, (hh + 1) * HEAD_DIM)
        q = q_ref[:, sl]
        kc = ck_ref[hh].astype(BF16)
        vc = cv_ref[hh].astype(BF16)
        kn = kn_ref[:, sl]
        vn = vn_ref[:, sl]
        c_row = c_ref[0, hh:hh + 1, :]
        c_first = c_row[:, past:past + 1]
        s_c = lax.dot_general(q, kc, (((1,), (1,)), ((), ())), preferred_element_type=F32) * ATTN_SCALE
        s_c = s_c + (c_first - c_row[:, :past])
        s_n = lax.dot_general(q, kn, (((1,), (1,)), ((), ())), preferred_element_type=F32) * ATTN_SCALE
        s_n = jnp.where(causal, s_n + (c_first - c_row[:, past:]), NEG_INF)
        m = jnp.maximum(jnp.max(s_c, axis=1, keepdims=True), jnp.max(s_n, axis=1, keepdims=True))
        p_c = jnp.exp(s_c - m)
        p_n = jnp.exp(s_n - m)
        l = jnp.sum(p_c, axis=1, keepdims=True) + jnp.sum(p_n, axis=1, keepdims=True)
        o = jnp.dot(p_c.astype(BF16), vc, preferred_element_type=F32)
        o = o + jnp.dot(p_n.astype(BF16), vn, preferred_element_type=F32)
        o_ref[:, sl] = (o / l).astype(o_ref.dtype)


def _fox_sample(o_cat, zq, zkv, cache_k, cache_v, cache_b0, c_all, row_blk0, dec_batch, t_new, *, hg):
    past = cache_k.shape[1]
    ng = H_A // hg
    wg = hg * HEAD_DIM
    q_col = RS_QA // wg
    k_col = ST_KA // wg
    v_col = ST_VA // wg
    o_col = OC_A // wg
    body, lead_specs, lead_args, aliases = _carried(
        o_cat, functools.partial(_fox_sample_kernel, hg=hg, past=past, t_new=t_new))
    return pl.pallas_call(
        body,
        grid=(dec_batch, ng),
        in_specs=lead_specs + [
            pl.BlockSpec((t_new, wg), lambda b, g: (row_blk0 + b, q_col + g)),
            pl.BlockSpec((t_new, wg), lambda b, g: (row_blk0 + b, k_col + g)),
            pl.BlockSpec((t_new, wg), lambda b, g: (row_blk0 + b, v_col + g)),
            pl.BlockSpec((hg, past, HEAD_DIM), lambda b, g: ((cache_b0 + b) * ng + g, 0, 0)),
            pl.BlockSpec((hg, past, HEAD_DIM), lambda b, g: ((cache_b0 + b) * ng + g, 0, 0)),
            pl.BlockSpec((1, hg, past + t_new), lambda b, g: (b * ng + g, 0, 0))],
        out_specs=pl.BlockSpec((t_new, wg), lambda b, g: (row_blk0 + b, o_col + g)),
        out_shape=jax.ShapeDtypeStruct((zq.shape[0], MIX_WIDTH), BF16),
        input_output_aliases=aliases,
        compiler_params=_cparams(("arbitrary", "arbitrary")),
        name="fox_sample",
    )(*lead_args, zq, zkv, zkv, cache_k, cache_v, c_all)


def _band_bias_kernel(rb_ref, op_ref, os_ref, *, tq, t_new, n_past):
    h = pl.program_id(0)
    ii = lax.broadcasted_iota(jnp.int32, (LANE, LANE), 0)
    jj = lax.broadcasted_iota(jnp.int32, (LANE, LANE), 1)
    idx_same = jj - ii + MAX_REL
    idx_prev = jnp.maximum(jj - ii - LANE, -MAX_REL) + MAX_REL

    def body(r, carry):
        t_same, t_prev = carry
        val = rb_ref[r * H_B + h]
        return jnp.where(idx_same == r, val, t_same), jnp.where(idx_prev == r, val, t_prev)

    zeros = jnp.zeros((LANE, LANE), F32)
    t_same, t_prev = lax.fori_loop(0, 2 * MAX_REL + 1, body, (zeros, zeros))
    far = jnp.full((LANE, LANE), rb_ref[h], F32)
    neg = jnp.full((LANE, LANE), NEG_INF, F32)

    def block(rel_blk):
        return t_same if rel_blk == 0 else t_prev if rel_blk == -1 else far

    for bi in range(tq // LANE):
        for bj in range(2 * tq // LANE):
            koff = bj * LANE - tq
            rel_blk = koff // LANE - bi
            qc = (bi * LANE + ii) // CHUNK
            kc = (koff + jj + tq) // CHUNK - tq // CHUNK
            vis = (kc <= qc) & (kc >= qc - BAND_CHUNKS)
            tile = neg if rel_blk > 0 else jnp.where(vis, block(rel_blk), NEG_INF)
            op_ref[0, bi * LANE:(bi + 1) * LANE, bj * LANE:(bj + 1) * LANE] = tile

    for j0 in range(0, n_past + t_new, LANE):
        w = min(LANE, n_past + t_new - j0)
        rel_blk = (j0 - n_past) // LANE
        qc = ii // CHUNK
        kc = (j0 + jj) // CHUNK - n_past // CHUNK
        vis = (kc <= qc) & (kc >= qc - BAND_CHUNKS)
        tile = jnp.where(vis, block(rel_blk), NEG_INF)
        os_ref[0, :, j0:j0 + w] = tile[:t_new, :w]


def _band_bias(rel_bias, *, tq, t_new, n_past):
    assert tq % LANE == 0 and n_past % LANE == 0 and t_new <= LANE and LANE % CHUNK == 0
    assert MAX_REL == LANE
    return pl.pallas_call(
        functools.partial(_band_bias_kernel, tq=tq, t_new=t_new, n_past=n_past),
        grid=(H_B,),
        in_specs=[pl.BlockSpec(memory_space=pltpu.SMEM)],
        out_specs=[pl.BlockSpec((1, tq, 2 * tq), lambda h: (h, 0, 0)),
                   pl.BlockSpec((1, t_new, n_past + t_new), lambda h: (h, 0, 0))],
        out_shape=[jax.ShapeDtypeStruct((H_B, tq, 2 * tq), F32),
                   jax.ShapeDtypeStruct((H_B, t_new, n_past + t_new), F32)],
        compiler_params=_cparams(("arbitrary",)),
        name="band_bias",
    )(rel_bias.reshape(-1))


def _band_prompt_kernel(q_ref, k_ref, v_ref, bias_ref, o_ref, *, tq):
    t = pl.program_id(2)
    q = q_ref[...]
    lo = pl.multiple_of(jnp.maximum(t - 1, 0) * tq, tq)
    hi = pl.multiple_of(t * tq, tq)
    nt = (((1,), (1,)), ((), ()))
    s_l = lax.dot_general(q, k_ref[pl.ds(lo, tq), :], nt, preferred_element_type=F32) * ATTN_SCALE
    s_l = jnp.where(t > 0, s_l + bias_ref[0, :, :tq], NEG_INF)
    s_r = lax.dot_general(q, k_ref[pl.ds(hi, tq), :], nt, preferred_element_type=F32) * ATTN_SCALE
    s_r = s_r + bias_ref[0, :, tq:]
    m = jnp.maximum(jnp.max(s_l, axis=1, keepdims=True), jnp.max(s_r, axis=1, keepdims=True))
    p_l = jnp.exp(s_l - m)
    p_r = jnp.exp(s_r - m)
    l = jnp.sum(p_l, axis=1, keepdims=True) + jnp.sum(p_r, axis=1, keepdims=True)
    o = jnp.dot(p_l.astype(BF16), v_ref[pl.ds(lo, tq), :], preferred_element_type=F32)
    o = o + jnp.dot(p_r.astype(BF16), v_ref[pl.ds(hi, tq), :], preferred_element_type=F32)
    o_ref[...] = (o / l).astype(o_ref.dtype)


def _band_prompt(o_cat, zq, zkv, bias_p, batch, seq, *, tq):
    nq = seq // tq
    q_col = RS_QB // HEAD_DIM
    k_col = ST_KB // HEAD_DIM
    v_col = ST_VB // HEAD_DIM
    o_col = OC_B // HEAD_DIM
    body, lead_specs, lead_args, aliases = _carried(o_cat, functools.partial(_band_prompt_kernel, tq=tq))
    return pl.pallas_call(
        body,
        grid=(H_B, batch, nq),
        in_specs=lead_specs + [
            pl.BlockSpec((tq, HEAD_DIM), lambda h, b, i: (b * nq + i, q_col + h)),
            pl.BlockSpec((seq, HEAD_DIM), lambda h, b, i: (b, k_col + h)),
            pl.BlockSpec((seq, HEAD_DIM), lambda h, b, i: (b, v_col + h)),
            pl.BlockSpec((1, tq, 2 * tq), lambda h, b, i: (h, 0, 0))],
        out_specs=pl.BlockSpec((tq, HEAD_DIM), lambda h, b, i: (b * nq + i, o_col + h)),
        out_shape=jax.ShapeDtypeStruct((zq.shape[0], MIX_WIDTH), BF16),
        input_output_aliases=aliases,
        compiler_params=_cparams(("arbitrary", "arbitrary", "arbitrary")),
        name="band_prompt",
    )(*lead_args, zq, zkv, zkv, bias_p)


def _band_sample_kernel(q_ref, kn_ref, vn_ref, ck_ref, cv_ref, bias_ref, o_ref, *, n_past):
    nt = (((1,), (1,)), ((), ()))
    for h in range(H_B):
        sl = slice(h * HEAD_DIM, (h + 1) * HEAD_DIM)
        q = q_ref[:, sl]
        kc = ck_ref[0, :, sl].astype(BF16)
        vc = cv_ref[0, :, sl].astype(BF16)
        s_c = lax.dot_general(q, kc, nt, preferred_element_type=F32) * ATTN_SCALE + bias_ref[h, :, :n_past]
        s_n = lax.dot_general(q, kn_ref[:, sl], nt, preferred_element_type=F32) * ATTN_SCALE + bias_ref[h, :, n_past:]
        m = jnp.maximum(jnp.max(s_c, axis=1, keepdims=True), jnp.max(s_n, axis=1, keepdims=True))
        p_c = jnp.exp(s_c - m)
        p_n = jnp.exp(s_n - m)
        l = jnp.sum(p_c, axis=1, keepdims=True) + jnp.sum(p_n, axis=1, keepdims=True)
        o = jnp.dot(p_c.astype(BF16), vc, preferred_element_type=F32)
        o = o + jnp.dot(p_n.astype(BF16), vn_ref[:, sl], preferred_element_type=F32)
        o_ref[:, sl] = (o / l).astype(o_ref.dtype)


def _band_sample(o_cat, zq, zkv, cache_k, cache_v, cache_b0, bias_s, row_blk0, dec_batch, t_new):
    n_past = cache_k.shape[1]
    q_col = RS_QB // W_B
    k_col = ST_KB // W_B
    v_col = ST_VB // W_B
    o_col = OC_B // W_B
    body, lead_specs, lead_args, aliases = _carried(o_cat, functools.partial(_band_sample_kernel, n_past=n_past))
    return pl.pallas_call(
        body,
        grid=(dec_batch,),
        in_specs=lead_specs + [
            pl.BlockSpec((t_new, W_B), lambda b: (row_blk0 + b, q_col)),
            pl.BlockSpec((t_new, W_B), lambda b: (row_blk0 + b, k_col)),
            pl.BlockSpec((t_new, W_B), lambda b: (row_blk0 + b, v_col)),
            pl.BlockSpec((1, n_past, W_B), lambda b: (cache_b0 + b, 0, 0)),
            pl.BlockSpec((1, n_past, W_B), lambda b: (cache_b0 + b, 0, 0)),
            pl.BlockSpec((H_B, t_new, n_past + t_new), lambda b: (0, 0, 0))],
        out_specs=pl.BlockSpec((t_new, W_B), lambda b: (row_blk0 + b, o_col)),
        out_shape=jax.ShapeDtypeStruct((zq.shape[0], MIX_WIDTH), BF16),
        input_output_aliases=aliases,
        compiler_params=_cparams(("arbitrary",)),
        name="band_sample",
    )(*lead_args, zq, zkv, zkv, cache_k, cache_v, bias_s)


GLA_EXP_CLAMP = 80.0


def _gla_kernel(q_ref, k_ref, v_ref, r_ref, g_ref, s0_ref, gn_ref, o_ref, so_ref, st_ref, *, chunk):
    n = pl.program_id(1)
    nc = pl.num_programs(1)

    @pl.when(n == 0)
    def _():
        for h in range(H_C):
            st_ref[h] = s0_ref[0, h].T

    ii = lax.broadcasted_iota(jnp.int32, (chunk, chunk), 0)
    jj = lax.broadcasted_iota(jnp.int32, (chunk, chunk), 1)
    tril = jj <= ii
    g_cum = jnp.dot(tril.astype(F32), g_ref[...], preferred_element_type=F32, precision=lax.Precision.HIGHEST)
    nt = (((1,), (1,)), ((), ()))
    tn = (((0,), (0,)), ((), ()))
    for h in range(H_C):
        ks = slice(h * DK_C, (h + 1) * DK_C)
        vs = slice(h * DV_C, (h + 1) * DV_C)
        gc = g_cum[:, ks]
        g_last = gc[chunk - 1:chunk, :]
        qf = q_ref[:, ks].astype(F32) * (DK_C ** -0.5)
        kf = k_ref[:, ks].astype(F32)
        vh = v_ref[:, vs]
        q_dec = (qf * jnp.exp(gc)).astype(BF16)
        k_inv = (kf * jnp.exp(jnp.minimum(-gc, GLA_EXP_CLAMP))).astype(BF16)
        k_dec = (kf * jnp.exp(g_last - gc)).astype(BF16)
        s_t = st_ref[h]
        o = lax.dot_general(q_dec, s_t.astype(BF16), nt, preferred_element_type=F32)
        a = lax.dot_general(q_dec, k_inv, nt, preferred_element_type=F32)
        a = jnp.where(tril, a, 0.0)
        o = o + jnp.dot(a.astype(BF16), vh, preferred_element_type=F32)
        st_ref[h] = s_t * jnp.exp(g_last) + lax.dot_general(vh, k_dec, tn, preferred_element_type=F32)
        on = o * lax.rsqrt(jnp.mean(o * o, axis=-1, keepdims=True) + RMS_EPS) * gn_ref[...]
        o_ref[:, vs] = (on * _silu(r_ref[:, vs].astype(F32))).astype(o_ref.dtype)

    @pl.when(n == nc - 1)
    def _():
        for h in range(H_C):
            so_ref[0, h] = st_ref[h].T


def _gla(o_cat, zq, g, s0, s0_b0, gnorm, row_blk0, n_seq, n_chunks, *, chunk):
    qc, kc, vc, rc = RS_QC // QK_C, RS_KC // QK_C, RS_VC // W_C, RS_RC // W_C
    o_col = OC_C // W_C
    row = lambda b, n: row_blk0 + b * n_chunks + n
    body, lead_specs, lead_args, aliases = _carried(o_cat, functools.partial(_gla_kernel, chunk=chunk))
    return pl.pallas_call(
        body,
        grid=(n_seq, n_chunks),
        in_specs=lead_specs + [
            pl.BlockSpec((chunk, QK_C), lambda b, n: (row(b, n), qc)),
            pl.BlockSpec((chunk, QK_C), lambda b, n: (row(b, n), kc)),
            pl.BlockSpec((chunk, W_C), lambda b, n: (row(b, n), vc)),
            pl.BlockSpec((chunk, W_C), lambda b, n: (row(b, n), rc)),
            pl.BlockSpec((chunk, QK_C), lambda b, n: (row(b, n), 0)),
            pl.BlockSpec((1, H_C, DK_C, DV_C), lambda b, n: (s0_b0 + b, 0, 0, 0)),
            pl.BlockSpec((1, DV_C), lambda b, n: (0, 0))],
        out_specs=[pl.BlockSpec((chunk, W_C), lambda b, n: (row(b, n), o_col)),
                   pl.BlockSpec((1, H_C, DK_C, DV_C), lambda b, n: (b, 0, 0, 0))],
        out_shape=[jax.ShapeDtypeStruct((zq.shape[0], MIX_WIDTH), BF16),
                   jax.ShapeDtypeStruct((n_seq, H_C, DK_C, DV_C), F32)],
        input_output_aliases=aliases,
        scratch_shapes=[pltpu.VMEM((H_C, DV_C, DK_C), F32)],
        compiler_params=_cparams(("arbitrary", "arbitrary")),
        name="gla",
    )(*lead_args, zq, zq, zq, zq, g, s0, gnorm.reshape(1, DV_C))


def _router_kernel(x_ref, w_ref, o_ref, *, n_experts):
    logits = jnp.dot(x_ref[...], w_ref[...], preferred_element_type=F32, precision=lax.Precision.HIGHEST)
    lane = lax.broadcasted_iota(jnp.int32, logits.shape, 1)
    lg = jnp.where(lane < n_experts, logits, NEG_INF)
    m0 = jnp.max(lg, axis=1, keepdims=True)
    i0 = jnp.min(jnp.where(lg == m0, lane, LANE), axis=1, keepdims=True)
    lg1 = jnp.where(lane == i0, NEG_INF, lg)
    m1 = jnp.max(lg1, axis=1, keepdims=True)
    i1 = jnp.min(jnp.where(lg1 == m1, lane, LANE), axis=1, keepdims=True)
    e1 = jnp.exp(m1 - m0)
    den = 1.0 + e1
    out = jnp.where(lane == 0, i0.astype(F32),
                    jnp.where(lane == 1, i1.astype(F32),
                              jnp.where(lane == 2, 1.0 / den, jnp.where(lane == 3, e1 / den, 0.0))))
    o_ref[...] = out


def _router(x_f32, w_router_pad, n_experts, *, tm):
    n, d = x_f32.shape
    return pl.pallas_call(
        functools.partial(_router_kernel, n_experts=n_experts),
        grid=(n // tm,),
        in_specs=[pl.BlockSpec((tm, d), lambda t: (t, 0)),
                  pl.BlockSpec((d, LANE), lambda t: (0, 0))],
        out_specs=pl.BlockSpec((tm, LANE), lambda t: (t, 0)),
        out_shape=jax.ShapeDtypeStruct((n, LANE), F32),
        compiler_params=_cparams(("arbitrary",)),
        name="moe_router",
    )(x_f32, w_router_pad)


def _gather_kernel(src_ref, x_hbm, o_ref, buf_ref, sem, *, tr):
    def row_copy(r):
        return pltpu.make_async_copy(x_hbm.at[pl.ds(src_ref[r], 1)], buf_ref.at[pl.ds(r, 1)], sem)

    def issue(r, c):
        row_copy(r).start()
        return c

    def drain(r, c):
        row_copy(r).wait()
        return c

    lax.fori_loop(0, tr, issue, 0)
    lax.fori_loop(0, tr, drain, 0)
    o_ref[...] = buf_ref[...].astype(o_ref.dtype)


def _gather_rows(x_f32, src, *, tr):
    d = x_f32.shape[1]
    rows = src.shape[0]
    return pl.pallas_call(
        functools.partial(_gather_kernel, tr=tr),
        grid=(rows // tr,),
        in_specs=[pl.BlockSpec((tr,), lambda t: (t,), memory_space=pltpu.SMEM),
                  pl.BlockSpec(memory_space=pl.ANY)],
        out_specs=pl.BlockSpec((tr, d), lambda t: (t, 0)),
        out_shape=jax.ShapeDtypeStruct((rows, d), BF16),
        scratch_shapes=[pltpu.VMEM((tr, d), F32), pltpu.SemaphoreType.DMA(())],
        compiler_params=_cparams(("arbitrary",)),
        name="moe_gather",
    )(src, x_f32)


def _combine_kernel(s0_ref, s1_ref, y_hbm, x_ref, g_ref, b_ref, of_ref, ob_ref, buf0, buf1, sem, *, tc, alpha):
    def copies(r):
        return (pltpu.make_async_copy(y_hbm.at[pl.ds(s0_ref[r], 1)], buf0.at[pl.ds(r, 1)], sem),
                pltpu.make_async_copy(y_hbm.at[pl.ds(s1_ref[r], 1)], buf1.at[pl.ds(r, 1)], sem))

    def issue(r, c):
        a, b = copies(r)
        a.start()
        b.start()
        return c

    def drain(r, c):
        a, b = copies(r)
        a.wait()
        b.wait()
        return c

    lax.fori_loop(0, tc, issue, 0)
    lax.fori_loop(0, tc, drain, 0)
    y = _layer_norm(alpha * x_ref[...] + (buf0[...] + buf1[...]), g_ref[...], b_ref[...])
    of_ref[...] = y
    ob_ref[...] = y.astype(BF16)


def _moe_combine(ys, slot0, slot1, x_f32, ln_g, ln_b, alpha, *, tc):
    n, d = x_f32.shape
    return pl.pallas_call(
        functools.partial(_combine_kernel, tc=tc, alpha=alpha),
        grid=(n // tc,),
        in_specs=[pl.BlockSpec((tc,), lambda t: (t,), memory_space=pltpu.SMEM),
                  pl.BlockSpec((tc,), lambda t: (t,), memory_space=pltpu.SMEM),
                  pl.BlockSpec(memory_space=pl.ANY),
                  pl.BlockSpec((tc, d), lambda t: (t, 0)),
                  pl.BlockSpec((1, d), lambda t: (0, 0)),
                  pl.BlockSpec((1, d), lambda t: (0, 0))],
        out_specs=[pl.BlockSpec((tc, d), lambda t: (t, 0)), pl.BlockSpec((tc, d), lambda t: (t, 0))],
        out_shape=[jax.ShapeDtypeStruct((n, d), F32), jax.ShapeDtypeStruct((n, d), BF16)],
        scratch_shapes=[pltpu.VMEM((tc, d), F32), pltpu.VMEM((tc, d), F32), pltpu.SemaphoreType.DMA(())],
        compiler_params=_cparams(("arbitrary",)),
        name="moe_combine",
    )(slot0, slot1, ys, x_f32, ln_g.reshape(1, d), ln_b.reshape(1, d))


def _moe_plan(route, n_experts, tm):
    n = route.shape[0]
    e = jnp.concatenate([route[:, 0], route[:, 1]]).astype(jnp.int32)
    gate = jnp.concatenate([route[:, 2], route[:, 3]])
    token = jnp.concatenate([jnp.arange(n, dtype=jnp.int32)] * TOP_K)
    onehot = (e[:, None] == jnp.arange(n_experts, dtype=jnp.int32)[None, :]).astype(jnp.int32)
    csum = jnp.cumsum(onehot, axis=0)
    rank = jnp.take_along_axis(csum, e[:, None], axis=1)[:, 0] - 1
    counts = csum[-1]
    tiles_per = (counts + tm - 1) // tm
    tile_end = jnp.cumsum(tiles_per)
    row_start = (tile_end - tiles_per) * tm
    dest = row_start[e] + rank
    n_tiles = (TOP_K * n + n_experts * (tm - 1)) // tm
    rows = n_tiles * tm
    src = jnp.zeros((rows,), jnp.int32).at[dest].set(token)
    gate_rows = jnp.zeros((rows,), F32).at[dest].set(gate)
    tile_expert = jnp.minimum(
        jnp.searchsorted(tile_end, jnp.arange(n_tiles, dtype=jnp.int32), side="right"), n_experts - 1).astype(jnp.int32)
    n_used = tile_end[-1:].astype(jnp.int32)
    return src, gate_rows.reshape(rows, 1), tile_expert, n_used, dest[:n], dest[n:]


def _ffn_dense(x_f32, x_bf, w1, w3, w2, ln_g, ln_b, alpha, cfg):
    n, d = x_bf.shape
    f = w1.shape[-1]
    tm, tf = cfg["ffn_tm"], cfg["ffn_tf"]
    te = jnp.zeros((n // tm,), jnp.int32)
    nu = jnp.full((1,), n // tm, jnp.int32)
    a, = _matmul_wres(x_bf, [w1[None], w3[None]], te, nu, tm=tm, tn=tf, col0=0, n_cols=f,
                      out_dtypes=[BF16], epilogue=_epi_swiglu, name="ffn_gate_up")
    return _proj_resid_ln(a, w2, x_f32, ln_g, ln_b, alpha, tm=cfg["ln_tm"], tk=cfg["down_tk"], name="ffn_down_ln")


def _cast_rows(total_rows, steps):
    r = 16
    while total_rows % r or total_rows // r > steps:
        r += 16
    return r


def _ffn_moe(x_f32, x_bf, w_router_pad, n_experts, w1, w3, w2_f32, ln_g, ln_b, alpha, cfg):
    n, d = x_bf.shape
    f = w1.shape[-1]
    tm, tf = cfg["moe_tm"], cfg["ffn_tf"]
    route = _router(x_f32, w_router_pad, n_experts, tm=cfg["row_tm"])
    src, gate_rows, tile_expert, n_used, slot0, slot1 = _moe_plan(route, n_experts, tm)
    xs = _gather_rows(x_f32, src, tr=cfg["gather_tr"])
    steps = (f // tf) * (xs.shape[0] // tm)
    a, w2 = _matmul_wres(xs, [w1, w3], tile_expert, n_used, tm=tm, tn=tf, col0=0, n_cols=f,
                         out_dtypes=[BF16], epilogue=_epi_swiglu, name="moe_gate_up",
                         cast_src=w2_f32.reshape(n_experts * f, d), cast_rows=_cast_rows(n_experts * f, steps))
    w2 = w2.reshape(n_experts, f, d)
    ys, = _matmul_acc(a, w2, tile_expert, n_used, [gate_rows],
                      [pl.BlockSpec((tm, 1), lambda t, k, te, nu: (t, 0))],
                      tm=tm, tk=cfg["moe_down_tk"], out_dtypes=[F32], epilogue=_epi_row_scale, name="moe_down")
    return _moe_combine(ys, slot0, slot1, x_f32, ln_g, ln_b, alpha, tc=cfg["combine_tc"])


def _split_w_in(w_in):
    sizes = (W_A, W_A, W_A, H_A, W_B, W_B, W_B, QK_C, QK_C, W_C, W_C, GATE_RANK)
    offs = np.concatenate([[0], np.cumsum(sizes)])
    w_t = w_in.T
    part = lambda i: w_t[offs[i]:offs[i + 1]]
    qa, ka, va, fa, qb, kb, vb, qc, kc, vc, rc, ac = [part(i) for i in range(12)]
    main = jnp.concatenate([ka, va, kb, vb, qc, kc, vc, rc, qa, qb], axis=0).astype(BF16)
    d = w_in.shape[0]
    fa_t = jnp.concatenate([fa, jnp.zeros((16 - H_A, d), F32)], axis=0).astype(BF16)
    ac_t = jnp.concatenate([ac, jnp.zeros((LANE - GATE_RANK, d), F32)], axis=0).astype(BF16)
    return main, fa_t, ac_t


def _mixer(x_bf, dims, layer, caches, params, cfg):
    batch, seq, dec_batch, t_new = dims
    fk_all, fv_all, bk_all, bv_all, gla_all, c_flogf, c_bk, c_bv = caches
    w_main, w_fa_t, w_ac_pad, b_forget, rel_bias, w_gate_up, b_gate, gla_norm_g = params
    n = x_bf.shape[0]
    n_p = batch * seq
    past = fk_all.shape[1]
    n_band = bk_all.shape[1]
    assert seq % CHUNK == 0 and t_new == CHUNK and past % CHUNK == 0 and n_band % LANE == 0
    assert n_band == min(BAND_PAST, past) and n_p % t_new == 0

    tm, tn = cfg["proj_tm"], cfg["proj_tn"]
    te = jnp.zeros((n // tm,), jnp.int32)
    nu = jnp.full((1,), n // tm, jnp.int32)
    zs_f32, zs = _matmul_wres(x_bf, [w_main[None]], te, nu, tm=tm, tn=tn, col0=0, n_cols=W_STATE,
                              out_dtypes=[F32, BF16], epilogue=_epi_store, name="proj_state", w_transposed=True)
    zr, = _matmul_wres(x_bf, [w_main[None]], te, nu, tm=tm, tn=tn, col0=W_STATE, n_cols=W_REST,
                       out_dtypes=[BF16], epilogue=_epi_store, name="proj_rest", w_transposed=True)

    b_col = jnp.zeros((16, 1), F32).at[:H_A, 0].set(b_forget)
    logf_t = _forget_logits(x_bf, w_fa_t, b_col, tm=cfg["row_tm"])
    w_up_pad = jnp.zeros((LANE, QK_C), F32).at[:GATE_RANK].set(w_gate_up).astype(BF16)
    g = _gla_gates(x_bf, w_ac_pad, w_up_pad, b_gate, tm=cfg["row_tm"])

    hp = cfg["fox_hp"]
    logf_bh = logf_t[:H_A, :n_p].reshape(H_A, batch, seq).transpose(1, 0, 2).reshape(batch * H_A, seq)
    c_p = _cumsum_last(logf_bh).reshape(batch * H_A // hp, hp, seq)
    o_cat = jnp.zeros((n, MIX_WIDTH), BF16)
    o_cat = _fox_prompt(o_cat, zr, zs, c_p, batch, seq, tq=cfg["fox_tq"], hp=hp)
    logf_s = logf_t[:H_A, n_p:].reshape(H_A, dec_batch, t_new).transpose(1, 0, 2)
    logf_all = jnp.concatenate([c_flogf.transpose(0, 2, 1), logf_s], axis=2)
    c_s = _cumsum_last(logf_all.reshape(dec_batch * H_A, past + t_new))
    hg = cfg["fox_hg"]
    c_s = c_s.reshape(dec_batch * (H_A // hg), hg, past + t_new)
    cache_b0 = layer * dec_batch
    o_cat = _fox_sample(o_cat, zr, zs, fk_all, fv_all, cache_b0, c_s, n_p // t_new, dec_batch, t_new, hg=hg)

    bias_p, bias_s = _band_bias(rel_bias, tq=cfg["band_tq"], t_new=t_new, n_past=n_band)
    o_cat = _band_prompt(o_cat, zr, zs, bias_p, batch, seq, tq=cfg["band_tq"])
    o_cat = _band_sample(o_cat, zr, zs, bk_all, bv_all, cache_b0, bias_s, n_p // t_new, dec_batch, t_new)

    s_zero = jnp.zeros((batch, H_C, DK_C, DV_C), F32)
    o_cat, s_p = _gla(o_cat, zr, g, s_zero, 0, gla_norm_g, 0, batch, seq // CHUNK, chunk=CHUNK)
    o_cat, s_s = _gla(o_cat, zr, g, gla_all, cache_b0, gla_norm_g, n_p // CHUNK, dec_batch, 1, chunk=CHUNK)

    ka, va = zs_f32[:, ST_KA:ST_KA + W_A], zs_f32[:, ST_VA:ST_VA + W_A]
    kb, vb = zs_f32[:, ST_KB:ST_KB + W_B], zs_f32[:, ST_VB:ST_VB + W_B]
    heads = lambda t, b, l, h: t.reshape(b, l, h, HEAD_DIM)
    nbp = min(BAND_PAST, seq)
    logf_p = logf_t[:H_A, :n_p].reshape(H_A, batch, seq).transpose(1, 2, 0)
    st_p = (heads(ka[:n_p], batch, seq, H_A), heads(va[:n_p], batch, seq, H_A), logf_p,
            heads(kb[:n_p], batch, seq, H_B)[:, seq - nbp:], heads(vb[:n_p], batch, seq, H_B)[:, seq - nbp:], s_p)
    kb_s, vb_s = heads(kb[n_p:], dec_batch, t_new, H_B), heads(vb[n_p:], dec_batch, t_new, H_B)
    st_s = (heads(ka[n_p:], dec_batch, t_new, H_A), heads(va[n_p:], dec_batch, t_new, H_A),
            logf_s.transpose(0, 2, 1),
            jnp.concatenate([c_bk, kb_s], axis=1)[:, t_new:], jnp.concatenate([c_bv, vb_s], axis=1)[:, t_new:], s_s)
    return o_cat, st_p, st_s


def _config(n, d, f, seq):
    return dict(
        proj_tm=_tile(n, 1024, 256), proj_tn=_tile(W_STATE, 1024, 512),
        row_tm=_tile(n, 512, 256),
        ffn_tm=_tile(n, 1024, 256), ffn_tf=_tile(f, 512, 128),
        ln_tm=_tile(n, 512, 128), down_tk=_tile(f, 512, 128), out_tk=_tile(MIX_WIDTH, 512, 128),
        moe_down_tk=_tile(f, 1024, 128),
        moe_tm=512 if n >= 4096 else 128, gather_tr=256 if n >= 4096 else 128,
        combine_tc=_tile(n, 256, 128),
        fox_tq=_tile(seq, 512, 128), fox_hp=2, fox_hg=4, band_tq=_tile(seq, 512, 128),
    )


def kernel(x_prompt, x_sample, cache_fox_k, cache_fox_v, cache_fox_logf, cache_band_k, cache_band_v, state_gla,
           w_in, b_forget, rel_bias, w_gate_up, b_gate, gla_norm_g, w_out,
           ln1_g, ln1_b, ln2_g, ln2_b, ffn_w1, ffn_w3, ffn_w2,
           moe_router, moe_w1, moe_w3, moe_w2):
    batch, seq, d = x_prompt.shape
    dec_batch, t_new, _ = x_sample.shape
    depth = w_in.shape[0]
    f = ffn_w1.shape[-1]
    n_experts = moe_router.shape[-1]
    n_p, n_s = batch * seq, dec_batch * t_new
    n = n_p + n_s
    alpha = (2.0 * depth) ** 0.25
    cfg = _config(n, d, f, seq)
    dims = (batch, seq, dec_batch, t_new)

    x = jnp.concatenate([x_prompt.reshape(n_p, d), x_sample.reshape(n_s, d)], axis=0)
    x_bf = x.astype(BF16)
    sp_all, ss_all = [], []
    past, n_band = cache_fox_k.shape[2], cache_band_k.shape[2]
    head_major = lambda c: c.transpose(0, 1, 3, 2, 4).reshape(depth * dec_batch * H_A, past, HEAD_DIM)
    fk_all, fv_all = head_major(cache_fox_k), head_major(cache_fox_v)
    bk_all = cache_band_k.reshape(depth * dec_batch, n_band, W_B)
    bv_all = cache_band_v.reshape(depth * dec_batch, n_band, W_B)
    gla_all = state_gla.reshape(depth * dec_batch, H_C, DK_C, DV_C)
    for i in range(depth):
        w_main, w_fa_t, w_ac_pad = _split_w_in(w_in[i])
        caches = (fk_all, fv_all, bk_all, bv_all, gla_all, cache_fox_logf[i], cache_band_k[i], cache_band_v[i])
        params = (w_main, w_fa_t, w_ac_pad, b_forget[i], rel_bias[i], w_gate_up[i], b_gate[i], gla_norm_g[i])
        o_cat, st_p, st_s = _mixer(x_bf, dims, i, caches, params, cfg)
        sp_all.append(st_p)
        ss_all.append(st_s)
        w_o = w_out[i]
        w_o = jnp.concatenate([w_o[W_A + W_B:], w_o[:W_A], w_o[W_A:W_A + W_B]], axis=0).astype(BF16)
        x, x_bf = _proj_resid_ln(o_cat, w_o, x, ln1_g[i], ln1_b[i], alpha,
                                 tm=cfg["ln_tm"], tk=cfg["out_tk"], name="out_proj_ln")
        j = i // 2
        if i % 2 == 0:
            x, x_bf = _ffn_dense(x, x_bf, ffn_w1[j].astype(BF16), ffn_w3[j].astype(BF16), ffn_w2[j].astype(BF16),
                                 ln2_g[i], ln2_b[i], alpha, cfg)
        else:
            w_router_pad = jnp.zeros((d, LANE), F32).at[:, :n_experts].set(moe_router[j])
            x, x_bf = _ffn_moe(x, x_bf, w_router_pad, n_experts, moe_w1[j].astype(BF16), moe_w3[j].astype(BF16),
                               moe_w2[j], ln2_g[i], ln2_b[i], alpha, cfg)
    stack = lambda states, k: jnp.stack([s[k] for s in states], axis=0)
    y_p = x[:n_p].reshape(batch, seq, d)
    y_s = x[n_p:].reshape(dec_batch, t_new, d)
    return (y_p, y_s) + tuple(stack(sp_all, k) for k in range(6)) + tuple(stack(ss_all, k) for k in range(6))
```

```python
import functools
import math

import jax
import jax.numpy as jnp
import numpy as np
from jax import lax
from jax.experimental import pallas as pl
from jax.experimental.pallas import tpu as pltpu

CHUNK = 64
HEAD_DIM = 128
H_A = 12
H_B = 8
H_C = 6
DK_C = 128
DV_C = 256
W_A = H_A * HEAD_DIM
W_B = H_B * HEAD_DIM
QK_C = H_C * DK_C
W_C = H_C * DV_C
GATE_RANK = 16
GLA_TAU = 16.0
BAND_CHUNKS = 8
BAND_PAST = BAND_CHUNKS * CHUNK
MAX_REL = 128
TOP_K = 2
LN_EPS = 1e-5
RMS_EPS = 1e-6
ATTN_SCALE = HEAD_DIM ** -0.5

LANE = 128
NEG_INF = float("-inf")
LOG2E = math.log2(math.e)
F32 = jnp.float32
BF16 = jnp.bfloat16
VMEM_LIMIT = 56 * 1024 * 1024

ST_KA, ST_VA, ST_KB, ST_VB = 0, W_A, 2 * W_A, 2 * W_A + W_B
W_STATE = 2 * W_A + 2 * W_B
RS_QC, RS_KC, RS_VC, RS_RC, RS_QA, RS_QB = 0, QK_C, 2 * QK_C, 2 * QK_C + W_C, 2 * QK_C + 2 * W_C, 2 * QK_C + 2 * W_C + W_A
W_REST = RS_QB + W_B
OC_C, OC_A, OC_B = 0, W_C, W_C + W_A
MIX_WIDTH = W_A + W_B + W_C


def _tile(dim, pref, mult):
    if dim <= pref:
        return dim
    t = (pref // mult) * mult
    while t > mult and dim % t:
        t -= mult
    assert dim % t == 0, (dim, pref, mult)
    return t


def _cparams(sem):
    return pltpu.CompilerParams(dimension_semantics=sem, vmem_limit_bytes=VMEM_LIMIT)


def _carry_output(body):
    def wrapped(_, *refs):
        body(*refs)
    return wrapped


def _carried(o_cat, body):
    return _carry_output(body), [pl.BlockSpec(memory_space=pl.ANY)], [o_cat], {0: 0}


def _log_sigmoid(x):
    return jnp.minimum(x, 0.0) - jnp.log(1.0 + jnp.exp(-jnp.abs(x)))


def _silu(x):
    return x / (1.0 + jnp.exp(-x))


def _layer_norm(x, g, b):
    mu = jnp.mean(x, axis=-1, keepdims=True)
    xc = x - mu
    var = jnp.mean(xc * xc, axis=-1, keepdims=True)
    return xc * lax.rsqrt(var + LN_EPS) * g + b


NT_DIMS = (((1,), (1,)), ((), ()))


def _wres_kernel(te_ref, nu_ref, x_ref, *refs, n_w, n_out, epilogue, w_transposed, cast_spans):
    w_refs = refs[:n_w]
    out_refs = refs[n_w + len(cast_spans):][:n_out]
    t = pl.program_id(1)

    @pl.when(t < nu_ref[0])
    def _():
        x = x_ref[...]
        if w_transposed:
            accs = [lax.dot_general(x, w[0], NT_DIMS, preferred_element_type=F32) for w in w_refs]
        else:
            accs = [jnp.dot(x, w[0], preferred_element_type=F32) for w in w_refs]
        epilogue(accs, out_refs)

    @pl.when(t >= nu_ref[0])
    def _():
        for o in out_refs:
            o[...] = jnp.zeros(o.shape, o.dtype)

    step = pl.program_id(0) * pl.num_programs(1) + t
    n_cast = len(cast_spans)
    for c, (start, n_blocks) in enumerate(cast_spans):
        src_ref, dst_ref = refs[n_w + c], refs[n_w + n_cast + n_out + c]

        @pl.when((step >= start) & (step < start + n_blocks))
        def _(src_ref=src_ref, dst_ref=dst_ref):
            dst_ref[...] = src_ref[...].astype(dst_ref.dtype)


def _matmul_wres(x, ws, tile_expert, n_used, *, tm, tn, col0, n_cols, out_dtypes, epilogue, name,
                 w_transposed=False, casts=()):
    rows, kdim = x.shape
    nt = rows // tm
    nj = n_cols // tn
    j0 = col0 // tn
    assert rows % tm == 0 and n_cols % tn == 0 and col0 % tn == 0
    if w_transposed:
        w_spec = pl.BlockSpec((1, tn, kdim), lambda j, t, te, nu: (te[t], j + j0, 0))
    else:
        w_spec = pl.BlockSpec((1, kdim, tn), lambda j, t, te, nu: (te[t], 0, j + j0))
    in_specs = [pl.BlockSpec((tm, kdim), lambda j, t, te, nu: (t, 0))] + [w_spec for _ in ws]
    out_specs = [pl.BlockSpec((tm, tn), lambda j, t, te, nu: (t, j)) for _ in out_dtypes]
    out_shape = [jax.ShapeDtypeStruct((rows, n_cols), dt) for dt in out_dtypes]
    args = [x, *ws]
    cast_spans = []
    for src, rows_per_step, start in casts:
        n_blocks = src.shape[0] // rows_per_step
        assert src.shape[0] % rows_per_step == 0 and start + n_blocks <= nj * nt
        cast_spans.append((start, n_blocks))

        def cast_index(j, t, te, nu, start=start, n_blocks=n_blocks):
            return jnp.clip(j * nt + t - start, 0, n_blocks - 1), 0

        in_specs.append(pl.BlockSpec((rows_per_step, src.shape[1]), cast_index))
        args.append(src)
    for (src, rows_per_step, _), spec in zip(casts, in_specs[1 + len(ws):]):
        out_specs.append(spec)
        out_shape.append(jax.ShapeDtypeStruct(src.shape, BF16))
    grid_spec = pltpu.PrefetchScalarGridSpec(
        num_scalar_prefetch=2, grid=(nj, nt), in_specs=in_specs, out_specs=out_specs)
    return pl.pallas_call(
        functools.partial(_wres_kernel, n_w=len(ws), n_out=len(out_dtypes), epilogue=epilogue,
                          w_transposed=w_transposed, cast_spans=tuple(cast_spans)),
        grid_spec=grid_spec,
        out_shape=out_shape,
        compiler_params=_cparams(("arbitrary", "arbitrary")),
        name=name,
    )(tile_expert, n_used, *args)


def _epi_store(accs, outs):
    for o in outs:
        o[...] = accs[0].astype(o.dtype)


def _epi_swiglu(accs, outs):
    outs[0][...] = (_silu(accs[0]) * accs[1]).astype(outs[0].dtype)


ROW_SLAB = 32


def _acc_kernel(te_ref, nu_ref, a_ref, w_ref, *refs, n_extra, epilogue):
    extra = refs[:n_extra]
    outs = refs[n_extra:]
    acc_ref = outs[0]
    t = pl.program_id(0)
    k = pl.program_id(1)
    nk = pl.num_programs(1)
    used = t < nu_ref[0]

    @pl.when(used & (k == 0))
    def _():
        acc_ref[...] = jnp.dot(a_ref[...], w_ref[0], preferred_element_type=F32)

    @pl.when(used & (k > 0))
    def _():
        acc_ref[...] += jnp.dot(a_ref[...], w_ref[0], preferred_element_type=F32)

    @pl.when(used & (k == nk - 1))
    def _():
        def slab(i, c):
            rows = pl.ds(pl.multiple_of(i * ROW_SLAB, ROW_SLAB), ROW_SLAB)
            epilogue(rows, extra, outs)
            return c

        lax.fori_loop(0, acc_ref.shape[0] // ROW_SLAB, slab, 0)

    @pl.when(jnp.logical_not(used) & (k == nk - 1))
    def _():
        for o in outs:
            o[...] = jnp.zeros(o.shape, o.dtype)


def _matmul_acc(a, w, tile_expert, n_used, extra, extra_specs, *, tm, tk, out_dtypes, epilogue, name):
    rows, kdim = a.shape
    n_out = w.shape[2]
    nt = rows // tm
    nk = kdim // tk
    assert rows % tm == 0 and kdim % tk == 0 and tm % ROW_SLAB == 0 and out_dtypes[0] == F32
    grid_spec = pltpu.PrefetchScalarGridSpec(
        num_scalar_prefetch=2,
        grid=(nt, nk),
        in_specs=[pl.BlockSpec((tm, tk), lambda t, k, te, nu: (t, k)),
                  pl.BlockSpec((1, tk, n_out), lambda t, k, te, nu: (te[t], k, 0))] + extra_specs,
        out_specs=[pl.BlockSpec((tm, n_out), lambda t, k, te, nu: (t, 0)) for _ in out_dtypes],
    )
    return pl.pallas_call(
        functools.partial(_acc_kernel, n_extra=len(extra), epilogue=epilogue),
        grid_spec=grid_spec,
        out_shape=[jax.ShapeDtypeStruct((rows, n_out), dt) for dt in out_dtypes],
        compiler_params=_cparams(("arbitrary", "arbitrary")),
        name=name,
    )(tile_expert, n_used, a, w, *extra)


def _epi_resid_ln(rows, extra, outs, *, alpha):
    x_ref, g_ref, b_ref = extra
    y = _layer_norm(alpha * x_ref[rows, :] + outs[0][rows, :], g_ref[...], b_ref[...])
    outs[0][rows, :] = y
    outs[1][rows, :] = y.astype(BF16)


def _epi_row_scale(rows, extra, outs):
    outs[0][rows, :] = outs[0][rows, :] * extra[0][rows, :]


def _proj_resid_ln(a, w, x_f32, ln_g, ln_b, alpha, *, tm, tk, name):
    rows = a.shape[0]
    d = w.shape[-1]
    te = jnp.zeros((rows // tm,), jnp.int32)
    nu = jnp.full((1,), rows // tm, jnp.int32)
    extra_specs = [pl.BlockSpec((tm, d), lambda t, k, te, nu: (t, 0)),
                   pl.BlockSpec((1, d), lambda t, k, te, nu: (0, 0)),
                   pl.BlockSpec((1, d), lambda t, k, te, nu: (0, 0))]
    return _matmul_acc(a, w.reshape((1,) + w.shape[-2:]), te, nu,
                       [x_f32, ln_g.reshape(1, d), ln_b.reshape(1, d)], extra_specs,
                       tm=tm, tk=tk, out_dtypes=[F32, BF16],
                       epilogue=functools.partial(_epi_resid_ln, alpha=alpha), name=name)


def _forget_kernel(wt_ref, b_ref, x_ref, o_ref):
    fa = lax.dot_general(wt_ref[...], x_ref[...], (((1,), (1,)), ((), ())), preferred_element_type=F32)
    o_ref[...] = _log_sigmoid(fa + b_ref[...])


def _forget_logits(x_bf, w_fa_t, b_col, *, tm):
    n, d = x_bf.shape
    hp = w_fa_t.shape[0]
    return pl.pallas_call(
        _forget_kernel,
        grid=(n // tm,),
        in_specs=[pl.BlockSpec((hp, d), lambda t: (0, 0)),
                  pl.BlockSpec((hp, 1), lambda t: (0, 0)),
                  pl.BlockSpec((tm, d), lambda t: (t, 0))],
        out_specs=pl.BlockSpec((hp, tm), lambda t: (0, t)),
        out_shape=jax.ShapeDtypeStruct((hp, n), F32),
        compiler_params=_cparams(("arbitrary",)),
        name="forget_logits",
    )(w_fa_t, b_col, x_bf)


def _gate_kernel(x_ref, wa_ref, wu_ref, b_ref, o_ref):
    ac = lax.dot_general(x_ref[...], wa_ref[...], NT_DIMS, preferred_element_type=F32)
    pre = jnp.dot(ac.astype(BF16), wu_ref[...], preferred_element_type=F32) + b_ref[...]
    o_ref[...] = _log_sigmoid(pre) * (1.0 / GLA_TAU)


def _gla_gates(x_bf, w_ac_pad, w_up_pad, b_gate, *, tm):
    n, d = x_bf.shape
    return pl.pallas_call(
        _gate_kernel,
        grid=(n // tm,),
        in_specs=[pl.BlockSpec((tm, d), lambda t: (t, 0)),
                  pl.BlockSpec((LANE, d), lambda t: (0, 0)),
                  pl.BlockSpec((LANE, QK_C), lambda t: (0, 0)),
                  pl.BlockSpec((1, QK_C), lambda t: (0, 0))],
        out_specs=pl.BlockSpec((tm, QK_C), lambda t: (t, 0)),
        out_shape=jax.ShapeDtypeStruct((n, QK_C), F32),
        compiler_params=_cparams(("arbitrary",)),
        name="gla_gates",
    )(x_bf, w_ac_pad, w_up_pad, b_gate.reshape(1, QK_C))


def _cumsum_kernel(x_ref, o_ref, *, t_len):
    rows = x_ref.shape[0]
    ii = lax.broadcasted_iota(jnp.int32, (LANE, LANE), 0)
    jj = lax.broadcasted_iota(jnp.int32, (LANE, LANE), 1)
    upper = (ii <= jj).astype(F32)
    carry = jnp.zeros((rows, 1), F32)
    for c0 in range(0, t_len, LANE):
        w = min(LANE, t_len - c0)
        blk = x_ref[:, c0:c0 + w]
        cs = jnp.dot(blk, upper[:w, :w], preferred_element_type=F32, precision=lax.Precision.HIGHEST) + carry
        o_ref[:, c0:c0 + w] = cs
        carry = cs[:, w - 1:w]


def _cumsum_last(x):
    rows, t_len = x.shape
    rb = 8
    assert rows % rb == 0
    return pl.pallas_call(
        functools.partial(_cumsum_kernel, t_len=t_len),
        grid=(rows // rb,),
        in_specs=[pl.BlockSpec((rb, t_len), lambda r: (r, 0))],
        out_specs=pl.BlockSpec((rb, t_len), lambda r: (r, 0)),
        out_shape=jax.ShapeDtypeStruct((rows, t_len), F32),
        compiler_params=_cparams(("arbitrary",)),
        name="cumsum_time",
    )(x)


def _fox_prompt_kernel(q_ref, k_ref, v_ref, c_ref, o_ref, *, tq, hp):
    qi = pl.program_id(2)
    q0 = pl.multiple_of(qi * tq, tq)
    heads = [slice(h * HEAD_DIM, (h + 1) * HEAD_DIM) for h in range(hp)]
    qs = [q_ref[:, sl] for sl in heads]
    c_first = [c_ref[0, h:h + 1, pl.ds(q0, LANE)][:, :1] for h in range(hp)]

    def step(k0, carry, masked):
        out = []
        for h, sl in enumerate(heads):
            m, l, acc = carry[h]
            kb = k_ref[pl.ds(k0, tq), sl]
            vb = v_ref[pl.ds(k0, tq), sl]
            s = lax.dot_general(qs[h], kb, NT_DIMS, preferred_element_type=F32) * (ATTN_SCALE * LOG2E)
            s = s + (c_first[h] - c_ref[0, h:h + 1, pl.ds(k0, tq)]) * LOG2E
            if masked:
                ii = lax.broadcasted_iota(jnp.int32, (tq, tq), 0)
                jj = lax.broadcasted_iota(jnp.int32, (tq, tq), 1)
                s = jnp.where(jj <= ii, s, NEG_INF)
            m_new = jnp.maximum(m, jnp.max(s, axis=1, keepdims=True))
            a = jnp.exp2(m - m_new)
            p = jnp.exp2(s - m_new)
            l = a * l + jnp.sum(p, axis=1, keepdims=True)
            acc = a * acc + jnp.dot(p.astype(BF16), vb, preferred_element_type=F32)
            out.append((m_new, l, acc))
        return tuple(out)

    init = tuple((jnp.full((tq, 1), NEG_INF, F32), jnp.zeros((tq, 1), F32), jnp.zeros((tq, HEAD_DIM), F32))
                 for _ in heads)
    carry = lax.fori_loop(0, qi, lambda kj, c: step(pl.multiple_of(kj * tq, tq), c, False), init)
    carry = step(q0, carry, True)
    for (m, l, acc), sl in zip(carry, heads):
        o_ref[:, sl] = (acc / l).astype(o_ref.dtype)


def _fox_prompt(o_cat, zq, zkv, c_rows, batch, seq, *, tq, hp):
    nq = seq // tq
    wg = hp * HEAD_DIM
    ng = H_A // hp
    q_col, k_col, v_col, o_col = RS_QA // wg, ST_KA // wg, ST_VA // wg, OC_A // wg
    body, lead_specs, lead_args, aliases = _carried(o_cat, functools.partial(_fox_prompt_kernel, tq=tq, hp=hp))
    return pl.pallas_call(
        body,
        grid=(batch, ng, nq),
        in_specs=lead_specs + [
            pl.BlockSpec((tq, wg), lambda b, g, i: (b * nq + i, q_col + g)),
            pl.BlockSpec((seq, wg), lambda b, g, i: (b, k_col + g)),
            pl.BlockSpec((seq, wg), lambda b, g, i: (b, v_col + g)),
            pl.BlockSpec((1, hp, seq), lambda b, g, i: (b * ng + g, 0, 0))],
        out_specs=pl.BlockSpec((tq, wg), lambda b, g, i: (b * nq + i, o_col + g)),
        out_shape=jax.ShapeDtypeStruct((zq.shape[0], MIX_WIDTH), BF16),
        input_output_aliases=aliases,
        compiler_params=_cparams(("arbitrary", "arbitrary", "arbitrary")),
        name="fox_prompt",
    )(*lead_args, zq, zkv, zkv, c_rows)


def _fox_sample_kernel(q_ref, kn_ref, vn_ref, ck_ref, cv_ref, c_ref, o_ref, *, hg, past, t_new):
    ii = lax.broadcasted_iota(jnp.int32, (t_new, t_new), 0)
    jj = lax.broadcasted_iota(jnp.int32, (t_new, t_new), 1)
    causal = jj <= ii
    for hh in range(hg):
        sl = slice(hh * HEAD_DIM, (hh + 1) * HEAD_DIM)
        q = q_ref[:, sl]
        kc = ck_ref[hh].astype(BF16)
        vc = cv_ref[hh].astype(BF16)
        kn = kn_ref[:, sl]
        vn = vn_ref[:, sl]
        c_row = c_ref[0, hh:hh + 1, :]
        c_first = c_row[:, past:past + 1]
        s_c = lax.dot_general(q, kc, (((1,), (1,)), ((), ())), preferred_element_type=F32) * ATTN_SCALE
        s_c = s_c + (c_first - c_row[:, :past])
        s_n = lax.dot_general(q, kn, (((1,), (1,)), ((), ())), preferred_element_type=F32) * ATTN_SCALE
        s_n = jnp.where(causal, s_n + (c_first - c_row[:, past:]), NEG_INF)
        m = jnp.maximum(jnp.max(s_c, axis=1, keepdims=True), jnp.max(s_n, axis=1, keepdims=True))
        p_c = jnp.exp(s_c - m)
        p_n = jnp.exp(s_n - m)
        l = jnp.sum(p_c, axis=1, keepdims=True) + jnp.sum(p_n, axis=1, keepdims=True)
        o = jnp.dot(p_c.astype(BF16), vc, preferred_element_type=F32)
        o = o + jnp.dot(p_n.astype(BF16), vn, preferred_element_type=F32)
        o_ref[:, sl] = (o / l).astype(o_ref.dtype)


def _fox_sample(o_cat, zq, zkv, cache_k, cache_v, cache_b0, c_all, row_blk0, dec_batch, t_new, *, hg):
    past = cache_k.shape[1]
    ng = H_A // hg
    wg = hg * HEAD_DIM
    q_col = RS_QA // wg
    k_col = ST_KA // wg
    v_col = ST_VA // wg
    o_col = OC_A // wg
    body, lead_specs, lead_args, aliases = _carried(
        o_cat, functools.partial(_fox_sample_kernel, hg=hg, past=past, t_new=t_new))
    return pl.pallas_call(
        body,
        grid=(dec_batch, ng),
        in_specs=lead_specs + [
            pl.BlockSpec((t_new, wg), lambda b, g: (row_blk0 + b, q_col + g)),
            pl.BlockSpec((t_new, wg), lambda b, g: (row_blk0 + b, k_col + g)),
            pl.BlockSpec((t_new, wg), lambda b, g: (row_blk0 + b, v_col + g)),
            pl.BlockSpec((hg, past, HEAD_DIM), lambda b, g: ((cache_b0 + b) * ng + g, 0, 0)),
            pl.BlockSpec((hg, past, HEAD_DIM), lambda b, g: ((cache_b0 + b) * ng + g, 0, 0)),
            pl.BlockSpec((1, hg, past + t_new), lambda b, g: (b * ng + g, 0, 0))],
        out_specs=pl.BlockSpec((t_new, wg), lambda b, g: (row_blk0 + b, o_col + g)),
        out_shape=jax.ShapeDtypeStruct((zq.shape[0], MIX_WIDTH), BF16),
        input_output_aliases=aliases,
        compiler_params=_cparams(("arbitrary", "arbitrary")),
        name="fox_sample",
    )(*lead_args, zq, zkv, zkv, cache_k, cache_v, c_all)


def _band_bias_kernel(rb_ref, op_ref, os_ref, *, tq, t_new, n_past):
    h = pl.program_id(0)
    ii = lax.broadcasted_iota(jnp.int32, (LANE, LANE), 0)
    jj = lax.broadcasted_iota(jnp.int32, (LANE, LANE), 1)
    idx_same = jj - ii + MAX_REL
    idx_prev = jnp.maximum(jj - ii - LANE, -MAX_REL) + MAX_REL

    def body(r, carry):
        t_same, t_prev = carry
        val = rb_ref[r * H_B + h]
        return jnp.where(idx_same == r, val, t_same), jnp.where(idx_prev == r, val, t_prev)

    zeros = jnp.zeros((LANE, LANE), F32)
    t_same, t_prev = lax.fori_loop(0, 2 * MAX_REL + 1, body, (zeros, zeros))
    far = jnp.full((LANE, LANE), rb_ref[h], F32)
    neg = jnp.full((LANE, LANE), NEG_INF, F32)

    def block(rel_blk):
        return t_same if rel_blk == 0 else t_prev if rel_blk == -1 else far

    for bi in range(tq // LANE):
        for bj in range(2 * tq // LANE):
            koff = bj * LANE - tq
            rel_blk = koff // LANE - bi
            qc = (bi * LANE + ii) // CHUNK
            kc = (koff + jj + tq) // CHUNK - tq // CHUNK
            vis = (kc <= qc) & (kc >= qc - BAND_CHUNKS)
            tile = neg if rel_blk > 0 else jnp.where(vis, block(rel_blk), NEG_INF)
            op_ref[0, bi * LANE:(bi + 1) * LANE, bj * LANE:(bj + 1) * LANE] = tile

    for j0 in range(0, n_past + t_new, LANE):
        w = min(LANE, n_past + t_new - j0)
        rel_blk = (j0 - n_past) // LANE
        qc = ii // CHUNK
        kc = (j0 + jj) // CHUNK - n_past // CHUNK
        vis = (kc <= qc) & (kc >= qc - BAND_CHUNKS)
        tile = jnp.where(vis, block(rel_blk), NEG_INF)
        os_ref[0, :, j0:j0 + w] = tile[:t_new, :w]


def _band_bias(rel_bias, *, tq, t_new, n_past):
    assert tq % LANE == 0 and n_past % LANE == 0 and t_new <= LANE and LANE % CHUNK == 0
    assert MAX_REL == LANE
    return pl.pallas_call(
        functools.partial(_band_bias_kernel, tq=tq, t_new=t_new, n_past=n_past),
        grid=(H_B,),
        in_specs=[pl.BlockSpec(memory_space=pltpu.SMEM)],
        out_specs=[pl.BlockSpec((1, tq, 2 * tq), lambda h: (h, 0, 0)),
                   pl.BlockSpec((1, t_new, n_past + t_new), lambda h: (h, 0, 0))],
        out_shape=[jax.ShapeDtypeStruct((H_B, tq, 2 * tq), F32),
                   jax.ShapeDtypeStruct((H_B, t_new, n_past + t_new), F32)],
        compiler_params=_cparams(("arbitrary",)),
        name="band_bias",
    )(rel_bias.reshape(-1))


def _band_prompt_kernel(q_ref, k_ref, v_ref, bias_ref, o_ref, *, tq):
    t = pl.program_id(2)
    q = q_ref[...]
    lo = pl.multiple_of(jnp.maximum(t - 1, 0) * tq, tq)
    hi = pl.multiple_of(t * tq, tq)
    nt = (((1,), (1,)), ((), ()))
    s_l = lax.dot_general(q, k_ref[pl.ds(lo, tq), :], nt, preferred_element_type=F32) * ATTN_SCALE
    s_l = jnp.where(t > 0, s_l + bias_ref[0, :, :tq], NEG_INF)
    s_r = lax.dot_general(q, k_ref[pl.ds(hi, tq), :], nt, preferred_element_type=F32) * ATTN_SCALE
    s_r = s_r + bias_ref[0, :, tq:]
    m = jnp.maximum(jnp.max(s_l, axis=1, keepdims=True), jnp.max(s_r, axis=1, keepdims=True))
    p_l = jnp.exp(s_l - m)
    p_r = jnp.exp(s_r - m)
    l = jnp.sum(p_l, axis=1, keepdims=True) + jnp.sum(p_r, axis=1, keepdims=True)
    o = jnp.dot(p_l.astype(BF16), v_ref[pl.ds(lo, tq), :], preferred_element_type=F32)
    o = o + jnp.dot(p_r.astype(BF16), v_ref[pl.ds(hi, tq), :], preferred_element_type=F32)
    o_ref[...] = (o / l).astype(o_ref.dtype)


def _band_prompt(o_cat, zq, zkv, bias_p, batch, seq, *, tq):
    nq = seq // tq
    q_col = RS_QB // HEAD_DIM
    k_col = ST_KB // HEAD_DIM
    v_col = ST_VB // HEAD_DIM
    o_col = OC_B // HEAD_DIM
    body, lead_specs, lead_args, aliases = _carried(o_cat, functools.partial(_band_prompt_kernel, tq=tq))
    return pl.pallas_call(
        body,
        grid=(H_B, batch, nq),
        in_specs=lead_specs + [
            pl.BlockSpec((tq, HEAD_DIM), lambda h, b, i: (b * nq + i, q_col + h)),
            pl.BlockSpec((seq, HEAD_DIM), lambda h, b, i: (b, k_col + h)),
            pl.BlockSpec((seq, HEAD_DIM), lambda h, b, i: (b, v_col + h)),
            pl.BlockSpec((1, tq, 2 * tq), lambda h, b, i: (h, 0, 0))],
        out_specs=pl.BlockSpec((tq, HEAD_DIM), lambda h, b, i: (b * nq + i, o_col + h)),
        out_shape=jax.ShapeDtypeStruct((zq.shape[0], MIX_WIDTH), BF16),
        input_output_aliases=aliases,
        compiler_params=_cparams(("arbitrary", "arbitrary", "arbitrary")),
        name="band_prompt",
    )(*lead_args, zq, zkv, zkv, bias_p)


def _band_sample_kernel(q_ref, kn_ref, vn_ref, ck_ref, cv_ref, bias_ref, o_ref, *, n_past):
    nt = (((1,), (1,)), ((), ()))
    for h in range(H_B):
        sl = slice(h * HEAD_DIM, (h + 1) * HEAD_DIM)
        q = q_ref[:, sl]
        kc = ck_ref[0, :, sl].astype(BF16)
        vc = cv_ref[0, :, sl].astype(BF16)
        s_c = lax.dot_general(q, kc, nt, preferred_element_type=F32) * ATTN_SCALE + bias_ref[h, :, :n_past]
        s_n = lax.dot_general(q, kn_ref[:, sl], nt, preferred_element_type=F32) * ATTN_SCALE + bias_ref[h, :, n_past:]
        m = jnp.maximum(jnp.max(s_c, axis=1, keepdims=True), jnp.max(s_n, axis=1, keepdims=True))
        p_c = jnp.exp(s_c - m)
        p_n = jnp.exp(s_n - m)
        l = jnp.sum(p_c, axis=1, keepdims=True) + jnp.sum(p_n, axis=1, keepdims=True)
        o = jnp.dot(p_c.astype(BF16), vc, preferred_element_type=F32)
        o = o + jnp.dot(p_n.astype(BF16), vn_ref[:, sl], preferred_element_type=F32)
        o_ref[:, sl] = (o / l).astype(o_ref.dtype)


def _band_sample(o_cat, zq, zkv, cache_k, cache_v, cache_b0, bias_s, row_blk0, dec_batch, t_new):
    n_past = cache_k.shape[1]
    q_col = RS_QB // W_B
    k_col = ST_KB // W_B
    v_col = ST_VB // W_B
    o_col = OC_B // W_B
    body, lead_specs, lead_args, aliases = _carried(o_cat, functools.partial(_band_sample_kernel, n_past=n_past))
    return pl.pallas_call(
        body,
        grid=(dec_batch,),
        in_specs=lead_specs + [
            pl.BlockSpec((t_new, W_B), lambda b: (row_blk0 + b, q_col)),
            pl.BlockSpec((t_new, W_B), lambda b: (row_blk0 + b, k_col)),
            pl.BlockSpec((t_new, W_B), lambda b: (row_blk0 + b, v_col)),
            pl.BlockSpec((1, n_past, W_B), lambda b: (cache_b0 + b, 0, 0)),
            pl.BlockSpec((1, n_past, W_B), lambda b: (cache_b0 + b, 0, 0)),
            pl.BlockSpec((H_B, t_new, n_past + t_new), lambda b: (0, 0, 0))],
        out_specs=pl.BlockSpec((t_new, W_B), lambda b: (row_blk0 + b, o_col)),
        out_shape=jax.ShapeDtypeStruct((zq.shape[0], MIX_WIDTH), BF16),
        input_output_aliases=aliases,
        compiler_params=_cparams(("arbitrary",)),
        name="band_sample",
    )(*lead_args, zq, zkv, zkv, cache_k, cache_v, bias_s)


GLA_EXP_CLAMP = 80.0


def _gla_kernel(q_ref, k_ref, v_ref, r_ref, g_ref, s0_ref, gn_ref, o_ref, so_ref, st_ref, *, chunk):
    n = pl.program_id(1)
    nc = pl.num_programs(1)

    @pl.when(n == 0)
    def _():
        for h in range(H_C):
            st_ref[h] = s0_ref[0, h].T

    ii = lax.broadcasted_iota(jnp.int32, (chunk, chunk), 0)
    jj = lax.broadcasted_iota(jnp.int32, (chunk, chunk), 1)
    tril = jj <= ii
    g_cum = jnp.dot(tril.astype(F32), g_ref[...], preferred_element_type=F32, precision=lax.Precision.HIGHEST)
    nt = (((1,), (1,)), ((), ()))
    tn = (((0,), (0,)), ((), ()))
    for h in range(H_C):
        ks = slice(h * DK_C, (h + 1) * DK_C)
        vs = slice(h * DV_C, (h + 1) * DV_C)
        gc = g_cum[:, ks]
        g_last = gc[chunk - 1:chunk, :]
        qf = q_ref[:, ks].astype(F32) * (DK_C ** -0.5)
        kf = k_ref[:, ks].astype(F32)
        vh = v_ref[:, vs]
        q_dec = (qf * jnp.exp(gc)).astype(BF16)
        k_inv = (kf * jnp.exp(jnp.minimum(-gc, GLA_EXP_CLAMP))).astype(BF16)
        k_dec = (kf * jnp.exp(g_last - gc)).astype(BF16)
        s_t = st_ref[h]
        o = lax.dot_general(q_dec, s_t.astype(BF16), nt, preferred_element_type=F32)
        a = lax.dot_general(q_dec, k_inv, nt, preferred_element_type=F32)
        a = jnp.where(tril, a, 0.0)
        o = o + jnp.dot(a.astype(BF16), vh, preferred_element_type=F32)
        st_ref[h] = s_t * jnp.exp(g_last) + lax.dot_general(vh, k_dec, tn, preferred_element_type=F32)
        on = o * lax.rsqrt(jnp.mean(o * o, axis=-1, keepdims=True) + RMS_EPS) * gn_ref[...]
        o_ref[:, vs] = (on * _silu(r_ref[:, vs].astype(F32))).astype(o_ref.dtype)

    @pl.when(n == nc - 1)
    def _():
        for h in range(H_C):
            so_ref[0, h] = st_ref[h].T


def _gla(o_cat, zq, g, s0, s0_b0, gnorm, row_blk0, n_seq, n_chunks, *, chunk):
    qc, kc, vc, rc = RS_QC // QK_C, RS_KC // QK_C, RS_VC // W_C, RS_RC // W_C
    o_col = OC_C // W_C
    row = lambda b, n: row_blk0 + b * n_chunks + n
    body, lead_specs, lead_args, aliases = _carried(o_cat, functools.partial(_gla_kernel, chunk=chunk))
    return pl.pallas_call(
        body,
        grid=(n_seq, n_chunks),
        in_specs=lead_specs + [
            pl.BlockSpec((chunk, QK_C), lambda b, n: (row(b, n), qc)),
            pl.BlockSpec((chunk, QK_C), lambda b, n: (row(b, n), kc)),
            pl.BlockSpec((chunk, W_C), lambda b, n: (row(b, n), vc)),
            pl.BlockSpec((chunk, W_C), lambda b, n: (row(b, n), rc)),
            pl.BlockSpec((chunk, QK_C), lambda b, n: (row(b, n), 0)),
            pl.BlockSpec((1, H_C, DK_C, DV_C), lambda b, n: (s0_b0 + b, 0, 0, 0)),
            pl.BlockSpec((1, DV_C), lambda b, n: (0, 0))],
        out_specs=[pl.BlockSpec((chunk, W_C), lambda b, n: (row(b, n), o_col)),
                   pl.BlockSpec((1, H_C, DK_C, DV_C), lambda b, n: (b, 0, 0, 0))],
        out_shape=[jax.ShapeDtypeStruct((zq.shape[0], MIX_WIDTH), BF16),
                   jax.ShapeDtypeStruct((n_seq, H_C, DK_C, DV_C), F32)],
        input_output_aliases=aliases,
        scratch_shapes=[pltpu.VMEM((H_C, DV_C, DK_C), F32)],
        compiler_params=_cparams(("arbitrary", "arbitrary")),
        name="gla",
    )(*lead_args, zq, zq, zq, zq, g, s0, gnorm.reshape(1, DV_C))


def _router_kernel(x_ref, w_ref, o_ref, *, n_experts):
    logits = jnp.dot(x_ref[...], w_ref[...], preferred_element_type=F32, precision=lax.Precision.HIGHEST)
    lane = lax.broadcasted_iota(jnp.int32, logits.shape, 1)
    lg = jnp.where(lane < n_experts, logits, NEG_INF)
    m0 = jnp.max(lg, axis=1, keepdims=True)
    i0 = jnp.min(jnp.where(lg == m0, lane, LANE), axis=1, keepdims=True)
    lg1 = jnp.where(lane == i0, NEG_INF, lg)
    m1 = jnp.max(lg1, axis=1, keepdims=True)
    i1 = jnp.min(jnp.where(lg1 == m1, lane, LANE), axis=1, keepdims=True)
    e1 = jnp.exp(m1 - m0)
    den = 1.0 + e1
    out = jnp.where(lane == 0, i0.astype(F32),
                    jnp.where(lane == 1, i1.astype(F32),
                              jnp.where(lane == 2, 1.0 / den, jnp.where(lane == 3, e1 / den, 0.0))))
    o_ref[...] = out


def _router(x_f32, w_router_pad, n_experts, *, tm):
    n, d = x_f32.shape
    return pl.pallas_call(
        functools.partial(_router_kernel, n_experts=n_experts),
        grid=(n // tm,),
        in_specs=[pl.BlockSpec((tm, d), lambda t: (t, 0)),
                  pl.BlockSpec((d, LANE), lambda t: (0, 0))],
        out_specs=pl.BlockSpec((tm, LANE), lambda t: (t, 0)),
        out_shape=jax.ShapeDtypeStruct((n, LANE), F32),
        compiler_params=_cparams(("arbitrary",)),
        name="moe_router",
    )(x_f32, w_router_pad)


def _gather_kernel(src_ref, x_hbm, o_ref, buf_ref, sem, *, tr):
    def row_copy(r):
        return pltpu.make_async_copy(x_hbm.at[pl.ds(src_ref[r], 1)], buf_ref.at[pl.ds(r, 1)], sem)

    def issue(r, c):
        row_copy(r).start()
        return c

    def drain(r, c):
        row_copy(r).wait()
        return c

    lax.fori_loop(0, tr, issue, 0)
    lax.fori_loop(0, tr, drain, 0)
    o_ref[...] = buf_ref[...].astype(o_ref.dtype)


def _gather_rows(x_f32, src, *, tr):
    d = x_f32.shape[1]
    rows = src.shape[0]
    return pl.pallas_call(
        functools.partial(_gather_kernel, tr=tr),
        grid=(rows // tr,),
        in_specs=[pl.BlockSpec((tr,), lambda t: (t,), memory_space=pltpu.SMEM),
                  pl.BlockSpec(memory_space=pl.ANY)],
        out_specs=pl.BlockSpec((tr, d), lambda t: (t, 0)),
        out_shape=jax.ShapeDtypeStruct((rows, d), BF16),
        scratch_shapes=[pltpu.VMEM((tr, d), F32), pltpu.SemaphoreType.DMA(())],
        compiler_params=_cparams(("arbitrary",)),
        name="moe_gather",
    )(src, x_f32)


def _combine_kernel(s0_ref, s1_ref, y_hbm, x_ref, g_ref, b_ref, of_ref, ob_ref, buf0, buf1, sem, *, tc, alpha):
    def copies(r):
        return (pltpu.make_async_copy(y_hbm.at[pl.ds(s0_ref[r], 1)], buf0.at[pl.ds(r, 1)], sem),
                pltpu.make_async_copy(y_hbm.at[pl.ds(s1_ref[r], 1)], buf1.at[pl.ds(r, 1)], sem))

    def issue(r, c):
        a, b = copies(r)
        a.start()
        b.start()
        return c

    def drain(r, c):
        a, b = copies(r)
        a.wait()
        b.wait()
        return c

    lax.fori_loop(0, tc, issue, 0)
    lax.fori_loop(0, tc, drain, 0)
    y = _layer_norm(alpha * x_ref[...] + (buf0[...] + buf1[...]), g_ref[...], b_ref[...])
    of_ref[...] = y
    ob_ref[...] = y.astype(BF16)


def _moe_combine(ys, slot0, slot1, x_f32, ln_g, ln_b, alpha, *, tc):
    n, d = x_f32.shape
    return pl.pallas_call(
        functools.partial(_combine_kernel, tc=tc, alpha=alpha),
        grid=(n // tc,),
        in_specs=[pl.BlockSpec((tc,), lambda t: (t,), memory_space=pltpu.SMEM),
                  pl.BlockSpec((tc,), lambda t: (t,), memory_space=pltpu.SMEM),
                  pl.BlockSpec(memory_space=pl.ANY),
                  pl.BlockSpec((tc, d), lambda t: (t, 0)),
                  pl.BlockSpec((1, d), lambda t: (0, 0)),
                  pl.BlockSpec((1, d), lambda t: (0, 0))],
        out_specs=[pl.BlockSpec((tc, d), lambda t: (t, 0)), pl.BlockSpec((tc, d), lambda t: (t, 0))],
        out_shape=[jax.ShapeDtypeStruct((n, d), F32), jax.ShapeDtypeStruct((n, d), BF16)],
        scratch_shapes=[pltpu.VMEM((tc, d), F32), pltpu.VMEM((tc, d), F32), pltpu.SemaphoreType.DMA(())],
        compiler_params=_cparams(("arbitrary",)),
        name="moe_combine",
    )(slot0, slot1, ys, x_f32, ln_g.reshape(1, d), ln_b.reshape(1, d))


def _moe_plan(route, n_experts, tm):
    n = route.shape[0]
    e = jnp.concatenate([route[:, 0], route[:, 1]]).astype(jnp.int32)
    gate = jnp.concatenate([route[:, 2], route[:, 3]])
    token = jnp.concatenate([jnp.arange(n, dtype=jnp.int32)] * TOP_K)
    onehot = (e[:, None] == jnp.arange(n_experts, dtype=jnp.int32)[None, :]).astype(jnp.int32)
    csum = jnp.cumsum(onehot, axis=0)
    rank = jnp.take_along_axis(csum, e[:, None], axis=1)[:, 0] - 1
    counts = csum[-1]
    tiles_per = (counts + tm - 1) // tm
    tile_end = jnp.cumsum(tiles_per)
    row_start = (tile_end - tiles_per) * tm
    dest = row_start[e] + rank
    n_tiles = (TOP_K * n + n_experts * (tm - 1)) // tm
    rows = n_tiles * tm
    src = jnp.zeros((rows,), jnp.int32).at[dest].set(token)
    gate_rows = jnp.zeros((rows,), F32).at[dest].set(gate)
    tile_expert = jnp.minimum(
        jnp.searchsorted(tile_end, jnp.arange(n_tiles, dtype=jnp.int32), side="right"), n_experts - 1).astype(jnp.int32)
    n_used = tile_end[-1:].astype(jnp.int32)
    return src, gate_rows.reshape(rows, 1), tile_expert, n_used, dest[:n], dest[n:]


def _cast_rows(total_rows, steps):
    r = 16
    while total_rows % r or total_rows // r > steps:
        r += 16
    return r


def _ffn_dense(x_f32, x_bf, w1, w3, w2, ln_g, ln_b, alpha, cfg, to_cast=()):
    n, d = x_bf.shape
    f = w1.shape[-1]
    tm, tf = cfg["ffn_tm"], cfg["ffn_tf"]
    te = jnp.zeros((n // tm,), jnp.int32)
    nu = jnp.full((1,), n // tm, jnp.int32)
    steps = (f // tf) * (n // tm)
    casts = []
    for c, arr in enumerate(to_cast):
        flat = arr.reshape(-1, arr.shape[-1])
        rows = _cast_rows(flat.shape[0], steps // len(to_cast) + steps // 64)
        start = c * (steps - flat.shape[0] // rows) // max(len(to_cast) - 1, 1)
        casts.append((flat, rows, start))
    a, *copies = _matmul_wres(x_bf, [w1[None], w3[None]], te, nu, tm=tm, tn=tf, col0=0, n_cols=f,
                              out_dtypes=[BF16], epilogue=_epi_swiglu, name="ffn_gate_up", casts=casts)
    y, y_bf = _proj_resid_ln(a, w2, x_f32, ln_g, ln_b, alpha, tm=cfg["ln_tm"], tk=cfg["down_tk"], name="ffn_down_ln")
    return y, y_bf, [cp.reshape(arr.shape) for cp, arr in zip(copies, to_cast)]


def _ffn_moe(x_f32, x_bf, w_router_pad, n_experts, w1, w3, w2_f32, ln_g, ln_b, alpha, cfg):
    n, d = x_bf.shape
    f = w1.shape[-1]
    tm, tf = cfg["moe_tm"], cfg["ffn_tf"]
    route = _router(x_f32, w_router_pad, n_experts, tm=cfg["row_tm"])
    src, gate_rows, tile_expert, n_used, slot0, slot1 = _moe_plan(route, n_experts, tm)
    xs = _gather_rows(x_f32, src, tr=cfg["gather_tr"])
    steps = (f // tf) * (xs.shape[0] // tm)
    a, w2 = _matmul_wres(xs, [w1, w3], tile_expert, n_used, tm=tm, tn=tf, col0=0, n_cols=f,
                         out_dtypes=[BF16], epilogue=_epi_swiglu, name="moe_gate_up",
                         casts=[(w2_f32.reshape(n_experts * f, d), _cast_rows(n_experts * f, steps), 0)])
    w2 = w2.reshape(n_experts, f, d)
    ys, = _matmul_acc(a, w2, tile_expert, n_used, [gate_rows],
                      [pl.BlockSpec((tm, 1), lambda t, k, te, nu: (t, 0))],
                      tm=tm, tk=cfg["moe_down_tk"], out_dtypes=[F32], epilogue=_epi_row_scale, name="moe_down")
    return _moe_combine(ys, slot0, slot1, x_f32, ln_g, ln_b, alpha, tc=cfg["combine_tc"])


def _split_w_in(w_in):
    sizes = (W_A, W_A, W_A, H_A, W_B, W_B, W_B, QK_C, QK_C, W_C, W_C, GATE_RANK)
    offs = np.concatenate([[0], np.cumsum(sizes)])
    w_t = w_in.T
    part = lambda i: w_t[offs[i]:offs[i + 1]]
    qa, ka, va, fa, qb, kb, vb, qc, kc, vc, rc, ac = [part(i) for i in range(12)]
    main = jnp.concatenate([ka, va, kb, vb, qc, kc, vc, rc, qa, qb], axis=0).astype(BF16)
    d = w_in.shape[0]
    fa_t = jnp.concatenate([fa, jnp.zeros((16 - H_A, d), F32)], axis=0).astype(BF16)
    ac_t = jnp.concatenate([ac, jnp.zeros((LANE - GATE_RANK, d), F32)], axis=0).astype(BF16)
    return main, fa_t, ac_t


def _mixer(x_bf, dims, layer, caches, params, cfg):
    batch, seq, dec_batch, t_new = dims
    fk_all, fv_all, bk_all, bv_all, gla_all, c_flogf, c_bk, c_bv = caches
    w_main, w_fa_t, w_ac_pad, b_forget, rel_bias, w_gate_up, b_gate, gla_norm_g = params
    n = x_bf.shape[0]
    n_p = batch * seq
    past = fk_all.shape[1]
    n_band = bk_all.shape[1]
    assert seq % CHUNK == 0 and t_new == CHUNK and past % CHUNK == 0 and n_band % LANE == 0
    assert n_band == min(BAND_PAST, past) and n_p % t_new == 0

    tm, tn = cfg["proj_tm"], cfg["proj_tn"]
    te = jnp.zeros((n // tm,), jnp.int32)
    nu = jnp.full((1,), n // tm, jnp.int32)
    zs_f32, zs = _matmul_wres(x_bf, [w_main[None]], te, nu, tm=tm, tn=tn, col0=0, n_cols=W_STATE,
                              out_dtypes=[F32, BF16], epilogue=_epi_store, name="proj_state", w_transposed=True)
    zr, = _matmul_wres(x_bf, [w_main[None]], te, nu, tm=tm, tn=tn, col0=W_STATE, n_cols=W_REST,
                       out_dtypes=[BF16], epilogue=_epi_store, name="proj_rest", w_transposed=True)

    b_col = jnp.zeros((16, 1), F32).at[:H_A, 0].set(b_forget)
    logf_t = _forget_logits(x_bf, w_fa_t, b_col, tm=cfg["row_tm"])
    w_up_pad = jnp.zeros((LANE, QK_C), F32).at[:GATE_RANK].set(w_gate_up).astype(BF16)
    g = _gla_gates(x_bf, w_ac_pad, w_up_pad, b_gate, tm=cfg["row_tm"])

    hp = cfg["fox_hp"]
    logf_bh = logf_t[:H_A, :n_p].reshape(H_A, batch, seq).transpose(1, 0, 2).reshape(batch * H_A, seq)
    c_p = _cumsum_last(logf_bh).reshape(batch * H_A // hp, hp, seq)
    o_cat = jnp.zeros((n, MIX_WIDTH), BF16)
    o_cat = _fox_prompt(o_cat, zr, zs, c_p, batch, seq, tq=cfg["fox_tq"], hp=hp)
    logf_s = logf_t[:H_A, n_p:].reshape(H_A, dec_batch, t_new).transpose(1, 0, 2)
    logf_all = jnp.concatenate([c_flogf.transpose(0, 2, 1), logf_s], axis=2)
    c_s = _cumsum_last(logf_all.reshape(dec_batch * H_A, past + t_new))
    hg = cfg["fox_hg"]
    c_s = c_s.reshape(dec_batch * (H_A // hg), hg, past + t_new)
    cache_b0 = layer * dec_batch
    o_cat = _fox_sample(o_cat, zr, zs, fk_all, fv_all, cache_b0, c_s, n_p // t_new, dec_batch, t_new, hg=hg)

    bias_p, bias_s = _band_bias(rel_bias, tq=cfg["band_tq"], t_new=t_new, n_past=n_band)
    o_cat = _band_prompt(o_cat, zr, zs, bias_p, batch, seq, tq=cfg["band_tq"])
    o_cat = _band_sample(o_cat, zr, zs, bk_all, bv_all, cache_b0, bias_s, n_p // t_new, dec_batch, t_new)

    s_zero = jnp.zeros((batch, H_C, DK_C, DV_C), F32)
    o_cat, s_p = _gla(o_cat, zr, g, s_zero, 0, gla_norm_g, 0, batch, seq // CHUNK, chunk=CHUNK)
    o_cat, s_s = _gla(o_cat, zr, g, gla_all, cache_b0, gla_norm_g, n_p // CHUNK, dec_batch, 1, chunk=CHUNK)

    ka, va = zs_f32[:, ST_KA:ST_KA + W_A], zs_f32[:, ST_VA:ST_VA + W_A]
    kb, vb = zs_f32[:, ST_KB:ST_KB + W_B], zs_f32[:, ST_VB:ST_VB + W_B]
    heads = lambda t, b, l, h: t.reshape(b, l, h, HEAD_DIM)
    nbp = min(BAND_PAST, seq)
    logf_p = logf_t[:H_A, :n_p].reshape(H_A, batch, seq).transpose(1, 2, 0)
    st_p = (heads(ka[:n_p], batch, seq, H_A), heads(va[:n_p], batch, seq, H_A), logf_p,
            heads(kb[:n_p], batch, seq, H_B)[:, seq - nbp:], heads(vb[:n_p], batch, seq, H_B)[:, seq - nbp:], s_p)
    kb_s, vb_s = heads(kb[n_p:], dec_batch, t_new, H_B), heads(vb[n_p:], dec_batch, t_new, H_B)
    st_s = (heads(ka[n_p:], dec_batch, t_new, H_A), heads(va[n_p:], dec_batch, t_new, H_A),
            logf_s.transpose(0, 2, 1),
            jnp.concatenate([c_bk, kb_s], axis=1)[:, t_new:], jnp.concatenate([c_bv, vb_s], axis=1)[:, t_new:], s_s)
    return o_cat, st_p, st_s


def _config(n, d, f, seq):
    return dict(
        proj_tm=_tile(n, 1024, 256), proj_tn=_tile(W_STATE, 1024, 512),
        row_tm=_tile(n, 512, 256),
        ffn_tm=_tile(n, 512, 256), ffn_tf=_tile(f, 512, 128),
        ln_tm=_tile(n, 512, 128), down_tk=_tile(f, 512, 128), out_tk=_tile(MIX_WIDTH, 512, 128),
        moe_down_tk=_tile(f, 1024, 128),
        moe_tm=512 if n >= 4096 else 128, gather_tr=256 if n >= 4096 else 128,
        combine_tc=_tile(n, 256, 128),
        fox_tq=_tile(seq, 512, 128), fox_hp=2, fox_hg=4, band_tq=_tile(seq, 512, 128),
    )


def kernel(x_prompt, x_sample, cache_fox_k, cache_fox_v, cache_fox_logf, cache_band_k, cache_band_v, state_gla,
           w_in, b_forget, rel_bias, w_gate_up, b_gate, gla_norm_g, w_out,
           ln1_g, ln1_b, ln2_g, ln2_b, ffn_w1, ffn_w3, ffn_w2,
           moe_router, moe_w1, moe_w3, moe_w2):
    batch, seq, d = x_prompt.shape
    dec_batch, t_new, _ = x_sample.shape
    depth = w_in.shape[0]
    f = ffn_w1.shape[-1]
    n_experts = moe_router.shape[-1]
    n_p, n_s = batch * seq, dec_batch * t_new
    n = n_p + n_s
    alpha = (2.0 * depth) ** 0.25
    cfg = _config(n, d, f, seq)
    dims = (batch, seq, dec_batch, t_new)

    x = jnp.concatenate([x_prompt.reshape(n_p, d), x_sample.reshape(n_s, d)], axis=0)
    x_bf = x.astype(BF16)
    sp_all, ss_all = [], []
    past, n_band = cache_fox_k.shape[2], cache_band_k.shape[2]
    head_major = lambda c: c.transpose(0, 1, 3, 2, 4).reshape(depth * dec_batch * H_A, past, HEAD_DIM)
    fk_all, fv_all = head_major(cache_fox_k), head_major(cache_fox_v)
    bk_all = cache_band_k.reshape(depth * dec_batch, n_band, W_B)
    bv_all = cache_band_v.reshape(depth * dec_batch, n_band, W_B)
    gla_all = state_gla.reshape(depth * dec_batch, H_C, DK_C, DV_C)
    for i in range(depth):
        w_main, w_fa_t, w_ac_pad = _split_w_in(w_in[i])
        caches = (fk_all, fv_all, bk_all, bv_all, gla_all, cache_fox_logf[i], cache_band_k[i], cache_band_v[i])
        params = (w_main, w_fa_t, w_ac_pad, b_forget[i], rel_bias[i], w_gate_up[i], b_gate[i], gla_norm_g[i])
        o_cat, st_p, st_s = _mixer(x_bf, dims, i, caches, params, cfg)
        sp_all.append(st_p)
        ss_all.append(st_s)
        w_o = w_out[i]
        w_o = jnp.concatenate([w_o[W_A + W_B:], w_o[:W_A], w_o[W_A:W_A + W_B]], axis=0).astype(BF16)
        x, x_bf = _proj_resid_ln(o_cat, w_o, x, ln1_g[i], ln1_b[i], alpha,
                                 tm=cfg["ln_tm"], tk=cfg["out_tk"], name="out_proj_ln")
        j = i // 2
        if i % 2 == 0:
            to_cast = (moe_w1[j], moe_w3[j]) if i + 1 < depth else ()
            x, x_bf, moe_gate_up_bf = _ffn_dense(x, x_bf, ffn_w1[j].astype(BF16), ffn_w3[j].astype(BF16),
                                                 ffn_w2[j].astype(BF16), ln2_g[i], ln2_b[i], alpha, cfg, to_cast)
        else:
            w_router_pad = jnp.zeros((d, LANE), F32).at[:, :n_experts].set(moe_router[j])
            x, x_bf = _ffn_moe(x, x_bf, w_router_pad, n_experts, *moe_gate_up_bf,
                               moe_w2[j], ln2_g[i], ln2_b[i], alpha, cfg)
    stack = lambda states, k: jnp.stack([s[k] for s in states], axis=0)
    y_p = x[:n_p].reshape(batch, seq, d)
    y_s = x[n_p:].reshape(dec_batch, t_new, d)
    return (y_p, y_s) + tuple(stack(sp_all, k) for k in range(6)) + tuple(stack(ss_all, k) for k in range(6))
```

```python
import functools
import math

import jax
import jax.numpy as jnp
import numpy as np
from jax import lax
from jax.experimental import pallas as pl
from jax.experimental.pallas import tpu as pltpu

CHUNK = 64
HEAD_DIM = 128
H_A = 12
H_B = 8
H_C = 6
DK_C = 128
DV_C = 256
W_A = H_A * HEAD_DIM
W_B = H_B * HEAD_DIM
QK_C = H_C * DK_C
W_C = H_C * DV_C
GATE_RANK = 16
GLA_TAU = 16.0
BAND_CHUNKS = 8
BAND_PAST = BAND_CHUNKS * CHUNK
MAX_REL = 128
TOP_K = 2
LN_EPS = 1e-5
RMS_EPS = 1e-6
ATTN_SCALE = HEAD_DIM ** -0.5

LANE = 128
NEG_INF = float("-inf")
LOG2E = math.log2(math.e)
F32 = jnp.float32
BF16 = jnp.bfloat16
VMEM_LIMIT = 56 * 1024 * 1024

ST_KA, ST_VA, ST_KB, ST_VB = 0, W_A, 2 * W_A, 2 * W_A + W_B
W_STATE = 2 * W_A + 2 * W_B
RS_QC, RS_KC, RS_VC, RS_RC, RS_QA, RS_QB = 0, QK_C, 2 * QK_C, 2 * QK_C + W_C, 2 * QK_C + 2 * W_C, 2 * QK_C + 2 * W_C + W_A
W_REST = RS_QB + W_B
OC_C, OC_A, OC_B = 0, W_C, W_C + W_A
MIX_WIDTH = W_A + W_B + W_C


def _tile(dim, pref, mult):
    if dim <= pref:
        return dim
    t = (pref // mult) * mult
    while t > mult and dim % t:
        t -= mult
    assert dim % t == 0, (dim, pref, mult)
    return t


def _cparams(sem):
    return pltpu.CompilerParams(dimension_semantics=sem, vmem_limit_bytes=VMEM_LIMIT)


def _carry_output(body):
    def wrapped(_, *refs):
        body(*refs)
    return wrapped


def _carried(o_cat, body):
    return _carry_output(body), [pl.BlockSpec(memory_space=pl.ANY)], [o_cat], {0: 0}


def _log_sigmoid(x):
    return jnp.minimum(x, 0.0) - jnp.log(1.0 + jnp.exp(-jnp.abs(x)))


def _silu(x):
    return x / (1.0 + jnp.exp(-x))


def _layer_norm(x, g, b):
    mu = jnp.mean(x, axis=-1, keepdims=True)
    xc = x - mu
    var = jnp.mean(xc * xc, axis=-1, keepdims=True)
    return xc * lax.rsqrt(var + LN_EPS) * g + b


NT_DIMS = (((1,), (1,)), ((), ()))


def _wres_kernel(te_ref, nu_ref, x_ref, *refs, n_w, n_out, epilogue, w_transposed, cast_spans):
    w_refs = refs[:n_w]
    out_refs = refs[n_w + len(cast_spans):][:n_out]
    t = pl.program_id(1)
    valid = nu_ref[t]
    half = x_ref.shape[0] // 2

    def compute(x, outs):
        if w_transposed:
            accs = [lax.dot_general(x, w[0], NT_DIMS, preferred_element_type=F32) for w in w_refs]
        else:
            accs = [jnp.dot(x, w[0], preferred_element_type=F32) for w in w_refs]
        epilogue(accs, outs)

    @pl.when(valid > half)
    def _():
        compute(x_ref[...], out_refs)

    @pl.when((valid > 0) & (valid <= half))
    def _():
        compute(x_ref[:half, :], [o.at[pl.ds(0, half)] for o in out_refs])
        for o in out_refs:
            o[half:, :] = jnp.zeros((o.shape[0] - half, o.shape[1]), o.dtype)

    @pl.when(valid == 0)
    def _():
        for o in out_refs:
            o[...] = jnp.zeros(o.shape, o.dtype)

    step = pl.program_id(0) * pl.num_programs(1) + t
    n_cast = len(cast_spans)
    for c, (start, n_blocks) in enumerate(cast_spans):
        src_ref, dst_ref = refs[n_w + c], refs[n_w + n_cast + n_out + c]

        @pl.when((step >= start) & (step < start + n_blocks))
        def _(src_ref=src_ref, dst_ref=dst_ref):
            dst_ref[...] = src_ref[...].astype(dst_ref.dtype)


def _matmul_wres(x, ws, tile_expert, tile_rows, *, tm, tn, col0, n_cols, out_dtypes, epilogue, name,
                 w_transposed=False, casts=()):
    rows, kdim = x.shape
    nt = rows // tm
    nj = n_cols // tn
    j0 = col0 // tn
    assert rows % tm == 0 and n_cols % tn == 0 and col0 % tn == 0
    if w_transposed:
        w_spec = pl.BlockSpec((1, tn, kdim), lambda j, t, te, nu: (te[t], j + j0, 0))
    else:
        w_spec = pl.BlockSpec((1, kdim, tn), lambda j, t, te, nu: (te[t], 0, j + j0))
    in_specs = [pl.BlockSpec((tm, kdim), lambda j, t, te, nu: (t, 0))] + [w_spec for _ in ws]
    out_specs = [pl.BlockSpec((tm, tn), lambda j, t, te, nu: (t, j)) for _ in out_dtypes]
    out_shape = [jax.ShapeDtypeStruct((rows, n_cols), dt) for dt in out_dtypes]
    args = [x, *ws]
    cast_spans = []
    for src, rows_per_step, start in casts:
        n_blocks = src.shape[0] // rows_per_step
        assert src.shape[0] % rows_per_step == 0 and start + n_blocks <= nj * nt
        cast_spans.append((start, n_blocks))

        def cast_index(j, t, te, nu, start=start, n_blocks=n_blocks):
            return jnp.clip(j * nt + t - start, 0, n_blocks - 1), 0

        in_specs.append(pl.BlockSpec((rows_per_step, src.shape[1]), cast_index))
        args.append(src)
    for (src, rows_per_step, _), spec in zip(casts, in_specs[1 + len(ws):]):
        out_specs.append(spec)
        out_shape.append(jax.ShapeDtypeStruct(src.shape, BF16))
    grid_spec = pltpu.PrefetchScalarGridSpec(
        num_scalar_prefetch=2, grid=(nj, nt), in_specs=in_specs, out_specs=out_specs)
    return pl.pallas_call(
        functools.partial(_wres_kernel, n_w=len(ws), n_out=len(out_dtypes), epilogue=epilogue,
                          w_transposed=w_transposed, cast_spans=tuple(cast_spans)),
        grid_spec=grid_spec,
        out_shape=out_shape,
        compiler_params=_cparams(("arbitrary", "arbitrary")),
        name=name,
    )(tile_expert, tile_rows, *args)


def _epi_store(accs, outs):
    for o in outs:
        o[...] = accs[0].astype(o.dtype)


def _epi_swiglu(accs, outs):
    outs[0][...] = (_silu(accs[0]) * accs[1]).astype(outs[0].dtype)


ROW_SLAB = 32
SLABS_PER_TRIP = 2


def _acc_kernel(te_ref, nu_ref, a_ref, w_ref, *refs, n_extra, epilogue):
    extra = refs[:n_extra]
    outs = refs[n_extra:]
    acc_ref = outs[0]
    t = pl.program_id(0)
    k = pl.program_id(1)
    nk = pl.num_programs(1)
    valid = nu_ref[t]
    half = a_ref.shape[0] // 2
    used = valid > 0
    full = valid > half
    part = used & jnp.logical_not(full)

    @pl.when(full & (k == 0))
    def _():
        acc_ref[...] = jnp.dot(a_ref[...], w_ref[0], preferred_element_type=F32)

    @pl.when(full & (k > 0))
    def _():
        acc_ref[...] += jnp.dot(a_ref[...], w_ref[0], preferred_element_type=F32)

    @pl.when(part & (k == 0))
    def _():
        acc_ref[:half, :] = jnp.dot(a_ref[:half, :], w_ref[0], preferred_element_type=F32)
        acc_ref[half:, :] = jnp.zeros((acc_ref.shape[0] - half, acc_ref.shape[1]), F32)

    @pl.when(part & (k > 0))
    def _():
        acc_ref[:half, :] += jnp.dot(a_ref[:half, :], w_ref[0], preferred_element_type=F32)

    @pl.when(used & (k == nk - 1))
    def _():
        def slabs(i, c):
            base = i * (SLABS_PER_TRIP * ROW_SLAB)
            rows = [pl.ds(pl.multiple_of(base + s * ROW_SLAB, ROW_SLAB), ROW_SLAB) for s in range(SLABS_PER_TRIP)]
            epilogue(rows, extra, outs)
            return c

        lax.fori_loop(0, acc_ref.shape[0] // (SLABS_PER_TRIP * ROW_SLAB), slabs, 0)

    @pl.when(jnp.logical_not(used) & (k == nk - 1))
    def _():
        for o in outs:
            o[...] = jnp.zeros(o.shape, o.dtype)


def _matmul_acc(a, w, tile_expert, tile_rows, extra, extra_specs, *, tm, tk, out_dtypes, epilogue, name):
    rows, kdim = a.shape
    n_out = w.shape[2]
    nt = rows // tm
    nk = kdim // tk
    assert rows % tm == 0 and kdim % tk == 0 and tm % ROW_SLAB == 0 and out_dtypes[0] == F32
    grid_spec = pltpu.PrefetchScalarGridSpec(
        num_scalar_prefetch=2,
        grid=(nt, nk),
        in_specs=[pl.BlockSpec((tm, tk), lambda t, k, te, nu: (t, k)),
                  pl.BlockSpec((1, tk, n_out), lambda t, k, te, nu: (te[t], k, 0))] + extra_specs,
        out_specs=[pl.BlockSpec((tm, n_out), lambda t, k, te, nu: (t, 0)) for _ in out_dtypes],
    )
    return pl.pallas_call(
        functools.partial(_acc_kernel, n_extra=len(extra), epilogue=epilogue),
        grid_spec=grid_spec,
        out_shape=[jax.ShapeDtypeStruct((rows, n_out), dt) for dt in out_dtypes],
        compiler_params=_cparams(("arbitrary", "arbitrary")),
        name=name,
    )(tile_expert, tile_rows, a, w, *extra)


def _epi_resid_ln(row_slabs, extra, outs, *, alpha):
    x_ref, g_ref, b_ref = extra
    hs = [alpha * x_ref[rows, :] + outs[0][rows, :] for rows in row_slabs]
    ys = [_layer_norm(h, g_ref[...], b_ref[...]) for h in hs]
    for rows, y in zip(row_slabs, ys):
        outs[0][rows, :] = y
        outs[1][rows, :] = y.astype(BF16)


def _epi_row_scale(row_slabs, extra, outs):
    ys = [outs[0][rows, :] * extra[0][rows, :] for rows in row_slabs]
    for rows, y in zip(row_slabs, ys):
        outs[0][rows, :] = y


def _proj_resid_ln(a, w, x_f32, ln_g, ln_b, alpha, *, tm, tk, name):
    rows = a.shape[0]
    d = w.shape[-1]
    te = jnp.zeros((rows // tm,), jnp.int32)
    nu = jnp.full((rows // tm,), tm, jnp.int32)
    extra_specs = [pl.BlockSpec((tm, d), lambda t, k, te, nu: (t, 0)),
                   pl.BlockSpec((1, d), lambda t, k, te, nu: (0, 0)),
                   pl.BlockSpec((1, d), lambda t, k, te, nu: (0, 0))]
    return _matmul_acc(a, w.reshape((1,) + w.shape[-2:]), te, nu,
                       [x_f32, ln_g.reshape(1, d), ln_b.reshape(1, d)], extra_specs,
                       tm=tm, tk=tk, out_dtypes=[F32, BF16],
                       epilogue=functools.partial(_epi_resid_ln, alpha=alpha), name=name)


def _forget_kernel(wt_ref, b_ref, x_ref, o_ref):
    fa = lax.dot_general(wt_ref[...], x_ref[...], (((1,), (1,)), ((), ())), preferred_element_type=F32)
    o_ref[...] = _log_sigmoid(fa + b_ref[...])


def _forget_logits(x_bf, w_fa_t, b_col, *, tm):
    n, d = x_bf.shape
    hp = w_fa_t.shape[0]
    return pl.pallas_call(
        _forget_kernel,
        grid=(n // tm,),
        in_specs=[pl.BlockSpec((hp, d), lambda t: (0, 0)),
                  pl.BlockSpec((hp, 1), lambda t: (0, 0)),
                  pl.BlockSpec((tm, d), lambda t: (t, 0))],
        out_specs=pl.BlockSpec((hp, tm), lambda t: (0, t)),
        out_shape=jax.ShapeDtypeStruct((hp, n), F32),
        compiler_params=_cparams(("arbitrary",)),
        name="forget_logits",
    )(w_fa_t, b_col, x_bf)


def _gate_kernel(x_ref, wa_ref, wu_ref, b_ref, o_ref):
    ac = lax.dot_general(x_ref[...], wa_ref[...], NT_DIMS, preferred_element_type=F32)
    pre = jnp.dot(ac.astype(BF16), wu_ref[...], preferred_element_type=F32) + b_ref[...]
    o_ref[...] = _log_sigmoid(pre) * (1.0 / GLA_TAU)


def _gla_gates(x_bf, w_ac_pad, w_up_pad, b_gate, *, tm):
    n, d = x_bf.shape
    return pl.pallas_call(
        _gate_kernel,
        grid=(n // tm,),
        in_specs=[pl.BlockSpec((tm, d), lambda t: (t, 0)),
                  pl.BlockSpec((LANE, d), lambda t: (0, 0)),
                  pl.BlockSpec((LANE, QK_C), lambda t: (0, 0)),
                  pl.BlockSpec((1, QK_C), lambda t: (0, 0))],
        out_specs=pl.BlockSpec((tm, QK_C), lambda t: (t, 0)),
        out_shape=jax.ShapeDtypeStruct((n, QK_C), F32),
        compiler_params=_cparams(("arbitrary",)),
        name="gla_gates",
    )(x_bf, w_ac_pad, w_up_pad, b_gate.reshape(1, QK_C))


def _cumsum_kernel(x_ref, o_ref, *, t_len):
    rows = x_ref.shape[0]
    ii = lax.broadcasted_iota(jnp.int32, (LANE, LANE), 0)
    jj = lax.broadcasted_iota(jnp.int32, (LANE, LANE), 1)
    upper = (ii <= jj).astype(F32)
    carry = jnp.zeros((rows, 1), F32)
    for c0 in range(0, t_len, LANE):
        w = min(LANE, t_len - c0)
        blk = x_ref[:, c0:c0 + w]
        cs = jnp.dot(blk, upper[:w, :w], preferred_element_type=F32, precision=lax.Precision.HIGHEST) + carry
        o_ref[:, c0:c0 + w] = cs
        carry = cs[:, w - 1:w]


def _cumsum_last(x):
    rows, t_len = x.shape
    rb = 8
    assert rows % rb == 0
    return pl.pallas_call(
        functools.partial(_cumsum_kernel, t_len=t_len),
        grid=(rows // rb,),
        in_specs=[pl.BlockSpec((rb, t_len), lambda r: (r, 0))],
        out_specs=pl.BlockSpec((rb, t_len), lambda r: (r, 0)),
        out_shape=jax.ShapeDtypeStruct((rows, t_len), F32),
        compiler_params=_cparams(("arbitrary",)),
        name="cumsum_time",
    )(x)


def _fox_prompt_kernel(q_ref, k_ref, v_ref, c_ref, o_ref, *, tq, hp):
    qi = pl.program_id(2)
    q0 = pl.multiple_of(qi * tq, tq)
    heads = [slice(h * HEAD_DIM, (h + 1) * HEAD_DIM) for h in range(hp)]
    qs = [q_ref[:, sl] for sl in heads]
    c_first = [c_ref[0, h:h + 1, pl.ds(q0, LANE)][:, :1] for h in range(hp)]

    def step(k0, carry, masked):
        out = []
        for h, sl in enumerate(heads):
            m, l, acc = carry[h]
            kb = k_ref[pl.ds(k0, tq), sl]
            vb = v_ref[pl.ds(k0, tq), sl]
            s = lax.dot_general(qs[h], kb, NT_DIMS, preferred_element_type=F32) * (ATTN_SCALE * LOG2E)
            s = s + (c_first[h] - c_ref[0, h:h + 1, pl.ds(k0, tq)]) * LOG2E
            if masked:
                ii = lax.broadcasted_iota(jnp.int32, (tq, tq), 0)
                jj = lax.broadcasted_iota(jnp.int32, (tq, tq), 1)
                s = jnp.where(jj <= ii, s, NEG_INF)
            m_new = jnp.maximum(m, jnp.max(s, axis=1, keepdims=True))
            a = jnp.exp2(m - m_new)
            p = jnp.exp2(s - m_new)
            l = a * l + jnp.sum(p, axis=1, keepdims=True)
            acc = a * acc + jnp.dot(p.astype(BF16), vb, preferred_element_type=F32)
            out.append((m_new, l, acc))
        return tuple(out)

    init = tuple((jnp.full((tq, 1), NEG_INF, F32), jnp.zeros((tq, 1), F32), jnp.zeros((tq, HEAD_DIM), F32))
                 for _ in heads)
    carry = lax.fori_loop(0, qi, lambda kj, c: step(pl.multiple_of(kj * tq, tq), c, False), init)
    carry = step(q0, carry, True)
    for (m, l, acc), sl in zip(carry, heads):
        o_ref[:, sl] = (acc / l).astype(o_ref.dtype)


def _fox_prompt(o_cat, zq, zkv, c_rows, batch, seq, *, tq, hp):
    nq = seq // tq
    wg = hp * HEAD_DIM
    ng = H_A // hp
    q_col, k_col, v_col, o_col = RS_QA // wg, ST_KA // wg, ST_VA // wg, OC_A // wg
    body, lead_specs, lead_args, aliases = _carried(o_cat, functools.partial(_fox_prompt_kernel, tq=tq, hp=hp))
    return pl.pallas_call(
        body,
        grid=(batch, ng, nq),
        in_specs=lead_specs + [
            pl.BlockSpec((tq, wg), lambda b, g, i: (b * nq + i, q_col + g)),
            pl.BlockSpec((seq, wg), lambda b, g, i: (b, k_col + g)),
            pl.BlockSpec((seq, wg), lambda b, g, i: (b, v_col + g)),
            pl.BlockSpec((1, hp, seq), lambda b, g, i: (b * ng + g, 0, 0))],
        out_specs=pl.BlockSpec((tq, wg), lambda b, g, i: (b * nq + i, o_col + g)),
        out_shape=jax.ShapeDtypeStruct((zq.shape[0], MIX_WIDTH), BF16),
        input_output_aliases=aliases,
        compiler_params=_cparams(("arbitrary", "arbitrary", "arbitrary")),
        name="fox_prompt",
    )(*lead_args, zq, zkv, zkv, c_rows)


def _fox_sample_kernel(q_ref, kn_ref, vn_ref, ck_ref, cv_ref, c_ref, o_ref, *, hg, past, t_new):
    ii = lax.broadcasted_iota(jnp.int32, (t_new, t_new), 0)
    jj = lax.broadcasted_iota(jnp.int32, (t_new, t_new), 1)
    causal = jj <= ii
    for hh in range(hg):
        sl = slice(hh * HEAD_DIM, (hh + 1) * HEAD_DIM)
        q = q_ref[:, sl]
        kc = ck_ref[hh].astype(BF16)
        vc = cv_ref[hh].astype(BF16)
        kn = kn_ref[:, sl]
        vn = vn_ref[:, sl]
        c_row = c_ref[0, hh:hh + 1, :]
        c_first = c_row[:, past:past + 1]
        s_c = lax.dot_general(q, kc, (((1,), (1,)), ((), ())), preferred_element_type=F32) * ATTN_SCALE
        s_c = s_c + (c_first - c_row[:, :past])
        s_n = lax.dot_general(q, kn, (((1,), (1,)), ((), ())), preferred_element_type=F32) * ATTN_SCALE
        s_n = jnp.where(causal, s_n + (c_first - c_row[:, past:]), NEG_INF)
        m = jnp.maximum(jnp.max(s_c, axis=1, keepdims=True), jnp.max(s_n, axis=1, keepdims=True))
        p_c = jnp.exp(s_c - m)
        p_n = jnp.exp(s_n - m)
        l = jnp.sum(p_c, axis=1, keepdims=True) + jnp.sum(p_n, axis=1, keepdims=True)
        o = jnp.dot(p_c.astype(BF16), vc, preferred_element_type=F32)
        o = o + jnp.dot(p_n.astype(BF16), vn, preferred_element_type=F32)
        o_ref[:, sl] = (o / l).astype(o_ref.dtype)


def _fox_sample(o_cat, zq, zkv, cache_k, cache_v, cache_b0, c_all, row_blk0, dec_batch, t_new, *, hg):
    past = cache_k.shape[1]
    ng = H_A // hg
    wg = hg * HEAD_DIM
    q_col = RS_QA // wg
    k_col = ST_KA // wg
    v_col = ST_VA // wg
    o_col = OC_A // wg
    body, lead_specs, lead_args, aliases = _carried(
        o_cat, functools.partial(_fox_sample_kernel, hg=hg, past=past, t_new=t_new))
    return pl.pallas_call(
        body,
        grid=(dec_batch, ng),
        in_specs=lead_specs + [
            pl.BlockSpec((t_new, wg), lambda b, g: (row_blk0 + b, q_col + g)),
            pl.BlockSpec((t_new, wg), lambda b, g: (row_blk0 + b, k_col + g)),
            pl.BlockSpec((t_new, wg), lambda b, g: (row_blk0 + b, v_col + g)),
            pl.BlockSpec((hg, past, HEAD_DIM), lambda b, g: ((cache_b0 + b) * ng + g, 0, 0)),
            pl.BlockSpec((hg, past, HEAD_DIM), lambda b, g: ((cache_b0 + b) * ng + g, 0, 0)),
            pl.BlockSpec((1, hg, past + t_new), lambda b, g: (b * ng + g, 0, 0))],
        out_specs=pl.BlockSpec((t_new, wg), lambda b, g: (row_blk0 + b, o_col + g)),
        out_shape=jax.ShapeDtypeStruct((zq.shape[0], MIX_WIDTH), BF16),
        input_output_aliases=aliases,
        compiler_params=_cparams(("arbitrary", "arbitrary")),
        name="fox_sample",
    )(*lead_args, zq, zkv, zkv, cache_k, cache_v, c_all)


def _band_bias_kernel(rb_ref, op_ref, os_ref, *, tq, t_new, n_past):
    h = pl.program_id(0)
    ii = lax.broadcasted_iota(jnp.int32, (LANE, LANE), 0)
    jj = lax.broadcasted_iota(jnp.int32, (LANE, LANE), 1)
    idx_same = jj - ii + MAX_REL
    idx_prev = jnp.maximum(jj - ii - LANE, -MAX_REL) + MAX_REL

    def body(r, carry):
        t_same, t_prev = carry
        val = rb_ref[r * H_B + h]
        return jnp.where(idx_same == r, val, t_same), jnp.where(idx_prev == r, val, t_prev)

    zeros = jnp.zeros((LANE, LANE), F32)
    t_same, t_prev = lax.fori_loop(0, 2 * MAX_REL + 1, body, (zeros, zeros))
    far = jnp.full((LANE, LANE), rb_ref[h], F32)
    neg = jnp.full((LANE, LANE), NEG_INF, F32)

    def block(rel_blk):
        return t_same if rel_blk == 0 else t_prev if rel_blk == -1 else far

    for bi in range(tq // LANE):
        for bj in range(2 * tq // LANE):
            koff = bj * LANE - tq
            rel_blk = koff // LANE - bi
            qc = (bi * LANE + ii) // CHUNK
            kc = (koff + jj + tq) // CHUNK - tq // CHUNK
            vis = (kc <= qc) & (kc >= qc - BAND_CHUNKS)
            tile = neg if rel_blk > 0 else jnp.where(vis, block(rel_blk), NEG_INF)
            op_ref[0, bi * LANE:(bi + 1) * LANE, bj * LANE:(bj + 1) * LANE] = tile

    for j0 in range(0, n_past + t_new, LANE):
        w = min(LANE, n_past + t_new - j0)
        rel_blk = (j0 - n_past) // LANE
        qc = ii // CHUNK
        kc = (j0 + jj) // CHUNK - n_past // CHUNK
        vis = (kc <= qc) & (kc >= qc - BAND_CHUNKS)
        tile = jnp.where(vis, block(rel_blk), NEG_INF)
        os_ref[0, :, j0:j0 + w] = tile[:t_new, :w]


def _band_bias(rel_bias, *, tq, t_new, n_past):
    assert tq % LANE == 0 and n_past % LANE == 0 and t_new <= LANE and LANE % CHUNK == 0
    assert MAX_REL == LANE
    return pl.pallas_call(
        functools.partial(_band_bias_kernel, tq=tq, t_new=t_new, n_past=n_past),
        grid=(H_B,),
        in_specs=[pl.BlockSpec(memory_space=pltpu.SMEM)],
        out_specs=[pl.BlockSpec((1, tq, 2 * tq), lambda h: (h, 0, 0)),
                   pl.BlockSpec((1, t_new, n_past + t_new), lambda h: (h, 0, 0))],
        out_shape=[jax.ShapeDtypeStruct((H_B, tq, 2 * tq), F32),
                   jax.ShapeDtypeStruct((H_B, t_new, n_past + t_new), F32)],
        compiler_params=_cparams(("arbitrary",)),
        name="band_bias",
    )(rel_bias.reshape(-1))


def _band_prompt_kernel(q_ref, k_ref, v_ref, bias_ref, o_ref, *, tq):
    t = pl.program_id(2)
    q = q_ref[...]
    lo = pl.multiple_of(jnp.maximum(t - 1, 0) * tq, tq)
    hi = pl.multiple_of(t * tq, tq)
    nt = (((1,), (1,)), ((), ()))
    s_l = lax.dot_general(q, k_ref[pl.ds(lo, tq), :], nt, preferred_element_type=F32) * ATTN_SCALE
    s_l = jnp.where(t > 0, s_l + bias_ref[0, :, :tq], NEG_INF)
    s_r = lax.dot_general(q, k_ref[pl.ds(hi, tq), :], nt, preferred_element_type=F32) * ATTN_SCALE
    s_r = s_r + bias_ref[0, :, tq:]
    m = jnp.maximum(jnp.max(s_l, axis=1, keepdims=True), jnp.max(s_r, axis=1, keepdims=True))
    p_l = jnp.exp(s_l - m)
    p_r = jnp.exp(s_r - m)
    l = jnp.sum(p_l, axis=1, keepdims=True) + jnp.sum(p_r, axis=1, keepdims=True)
    o = jnp.dot(p_l.astype(BF16), v_ref[pl.ds(lo, tq), :], preferred_element_type=F32)
    o = o + jnp.dot(p_r.astype(BF16), v_ref[pl.ds(hi, tq), :], preferred_element_type=F32)
    o_ref[...] = (o / l).astype(o_ref.dtype)


def _band_prompt(o_cat, zq, zkv, bias_p, batch, seq, *, tq):
    nq = seq // tq
    q_col = RS_QB // HEAD_DIM
    k_col = ST_KB // HEAD_DIM
    v_col = ST_VB // HEAD_DIM
    o_col = OC_B // HEAD_DIM
    body, lead_specs, lead_args, aliases = _carried(o_cat, functools.partial(_band_prompt_kernel, tq=tq))
    return pl.pallas_call(
        body,
        grid=(H_B, batch, nq),
        in_specs=lead_specs + [
            pl.BlockSpec((tq, HEAD_DIM), lambda h, b, i: (b * nq + i, q_col + h)),
            pl.BlockSpec((seq, HEAD_DIM), lambda h, b, i: (b, k_col + h)),
            pl.BlockSpec((seq, HEAD_DIM), lambda h, b, i: (b, v_col + h)),
            pl.BlockSpec((1, tq, 2 * tq), lambda h, b, i: (h, 0, 0))],
        out_specs=pl.BlockSpec((tq, HEAD_DIM), lambda h, b, i: (b * nq + i, o_col + h)),
        out_shape=jax.ShapeDtypeStruct((zq.shape[0], MIX_WIDTH), BF16),
        input_output_aliases=aliases,
        compiler_params=_cparams(("arbitrary", "arbitrary", "arbitrary")),
        name="band_prompt",
    )(*lead_args, zq, zkv, zkv, bias_p)


def _band_sample_kernel(q_ref, kn_ref, vn_ref, ck_ref, cv_ref, bias_ref, o_ref, *, n_past):
    nt = (((1,), (1,)), ((), ()))
    for h in range(H_B):
        sl = slice(h * HEAD_DIM, (h + 1) * HEAD_DIM)
        q = q_ref[:, sl]
        kc = ck_ref[0, :, sl].astype(BF16)
        vc = cv_ref[0, :, sl].astype(BF16)
        s_c = lax.dot_general(q, kc, nt, preferred_element_type=F32) * ATTN_SCALE + bias_ref[h, :, :n_past]
        s_n = lax.dot_general(q, kn_ref[:, sl], nt, preferred_element_type=F32) * ATTN_SCALE + bias_ref[h, :, n_past:]
        m = jnp.maximum(jnp.max(s_c, axis=1, keepdims=True), jnp.max(s_n, axis=1, keepdims=True))
        p_c = jnp.exp(s_c - m)
        p_n = jnp.exp(s_n - m)
        l = jnp.sum(p_c, axis=1, keepdims=True) + jnp.sum(p_n, axis=1, keepdims=True)
        o = jnp.dot(p_c.astype(BF16), vc, preferred_element_type=F32)
        o = o + jnp.dot(p_n.astype(BF16), vn_ref[:, sl], preferred_element_type=F32)
        o_ref[:, sl] = (o / l).astype(o_ref.dtype)


def _band_sample(o_cat, zq, zkv, cache_k, cache_v, cache_b0, bias_s, row_blk0, dec_batch, t_new):
    n_past = cache_k.shape[1]
    q_col = RS_QB // W_B
    k_col = ST_KB // W_B
    v_col = ST_VB // W_B
    o_col = OC_B // W_B
    body, lead_specs, lead_args, aliases = _carried(o_cat, functools.partial(_band_sample_kernel, n_past=n_past))
    return pl.pallas_call(
        body,
        grid=(dec_batch,),
        in_specs=lead_specs + [
            pl.BlockSpec((t_new, W_B), lambda b: (row_blk0 + b, q_col)),
            pl.BlockSpec((t_new, W_B), lambda b: (row_blk0 + b, k_col)),
            pl.BlockSpec((t_new, W_B), lambda b: (row_blk0 + b, v_col)),
            pl.BlockSpec((1, n_past, W_B), lambda b: (cache_b0 + b, 0, 0)),
            pl.BlockSpec((1, n_past, W_B), lambda b: (cache_b0 + b, 0, 0)),
            pl.BlockSpec((H_B, t_new, n_past + t_new), lambda b: (0, 0, 0))],
        out_specs=pl.BlockSpec((t_new, W_B), lambda b: (row_blk0 + b, o_col)),
        out_shape=jax.ShapeDtypeStruct((zq.shape[0], MIX_WIDTH), BF16),
        input_output_aliases=aliases,
        compiler_params=_cparams(("arbitrary",)),
        name="band_sample",
    )(*lead_args, zq, zkv, zkv, cache_k, cache_v, bias_s)


GLA_EXP_CLAMP = 80.0


def _gla_kernel(q_ref, k_ref, v_ref, r_ref, g_ref, s0_ref, gn_ref, o_ref, so_ref, st_ref, *, chunk):
    n = pl.program_id(1)
    nc = pl.num_programs(1)

    @pl.when(n == 0)
    def _():
        for h in range(H_C):
            st_ref[h] = s0_ref[0, h].T

    ii = lax.broadcasted_iota(jnp.int32, (chunk, chunk), 0)
    jj = lax.broadcasted_iota(jnp.int32, (chunk, chunk), 1)
    tril = jj <= ii
    g_cum = jnp.dot(tril.astype(F32), g_ref[...], preferred_element_type=F32, precision=lax.Precision.HIGHEST)
    nt = (((1,), (1,)), ((), ()))
    tn = (((0,), (0,)), ((), ()))
    for h in range(H_C):
        ks = slice(h * DK_C, (h + 1) * DK_C)
        vs = slice(h * DV_C, (h + 1) * DV_C)
        gc = g_cum[:, ks]
        g_last = gc[chunk - 1:chunk, :]
        qf = q_ref[:, ks].astype(F32) * (DK_C ** -0.5)
        kf = k_ref[:, ks].astype(F32)
        vh = v_ref[:, vs]
        q_dec = (qf * jnp.exp(gc)).astype(BF16)
        k_inv = (kf * jnp.exp(jnp.minimum(-gc, GLA_EXP_CLAMP))).astype(BF16)
        k_dec = (kf * jnp.exp(g_last - gc)).astype(BF16)
        s_t = st_ref[h]
        o = lax.dot_general(q_dec, s_t.astype(BF16), nt, preferred_element_type=F32)
        a = lax.dot_general(q_dec, k_inv, nt, preferred_element_type=F32)
        a = jnp.where(tril, a, 0.0)
        o = o + jnp.dot(a.astype(BF16), vh, preferred_element_type=F32)
        st_ref[h] = s_t * jnp.exp(g_last) + lax.dot_general(vh, k_dec, tn, preferred_element_type=F32)
        on = o * lax.rsqrt(jnp.mean(o * o, axis=-1, keepdims=True) + RMS_EPS) * gn_ref[...]
        o_ref[:, vs] = (on * _silu(r_ref[:, vs].astype(F32))).astype(o_ref.dtype)

    @pl.when(n == nc - 1)
    def _():
        for h in range(H_C):
            so_ref[0, h] = st_ref[h].T


def _gla(o_cat, zq, g, s0, s0_b0, gnorm, row_blk0, n_seq, n_chunks, *, chunk):
    qc, kc, vc, rc = RS_QC // QK_C, RS_KC // QK_C, RS_VC // W_C, RS_RC // W_C
    o_col = OC_C // W_C
    row = lambda b, n: row_blk0 + b * n_chunks + n
    body, lead_specs, lead_args, aliases = _carried(o_cat, functools.partial(_gla_kernel, chunk=chunk))
    return pl.pallas_call(
        body,
        grid=(n_seq, n_chunks),
        in_specs=lead_specs + [
            pl.BlockSpec((chunk, QK_C), lambda b, n: (row(b, n), qc)),
            pl.BlockSpec((chunk, QK_C), lambda b, n: (row(b, n), kc)),
            pl.BlockSpec((chunk, W_C), lambda b, n: (row(b, n), vc)),
            pl.BlockSpec((chunk, W_C), lambda b, n: (row(b, n), rc)),
            pl.BlockSpec((chunk, QK_C), lambda b, n: (row(b, n), 0)),
            pl.BlockSpec((1, H_C, DK_C, DV_C), lambda b, n: (s0_b0 + b, 0, 0, 0)),
            pl.BlockSpec((1, DV_C), lambda b, n: (0, 0))],
        out_specs=[pl.BlockSpec((chunk, W_C), lambda b, n: (row(b, n), o_col)),
                   pl.BlockSpec((1, H_C, DK_C, DV_C), lambda b, n: (b, 0, 0, 0))],
        out_shape=[jax.ShapeDtypeStruct((zq.shape[0], MIX_WIDTH), BF16),
                   jax.ShapeDtypeStruct((n_seq, H_C, DK_C, DV_C), F32)],
        input_output_aliases=aliases,
        scratch_shapes=[pltpu.VMEM((H_C, DV_C, DK_C), F32)],
        compiler_params=_cparams(("arbitrary", "arbitrary")),
        name="gla",
    )(*lead_args, zq, zq, zq, zq, g, s0, gnorm.reshape(1, DV_C))


def _router_kernel(x_ref, w_ref, o_ref, *, n_experts):
    logits = jnp.dot(x_ref[...], w_ref[...], preferred_element_type=F32, precision=lax.Precision.HIGHEST)
    lane = lax.broadcasted_iota(jnp.int32, logits.shape, 1)
    lg = jnp.where(lane < n_experts, logits, NEG_INF)
    m0 = jnp.max(lg, axis=1, keepdims=True)
    i0 = jnp.min(jnp.where(lg == m0, lane, LANE), axis=1, keepdims=True)
    lg1 = jnp.where(lane == i0, NEG_INF, lg)
    m1 = jnp.max(lg1, axis=1, keepdims=True)
    i1 = jnp.min(jnp.where(lg1 == m1, lane, LANE), axis=1, keepdims=True)
    e1 = jnp.exp(m1 - m0)
    den = 1.0 + e1
    out = jnp.where(lane == 0, i0.astype(F32),
                    jnp.where(lane == 1, i1.astype(F32),
                              jnp.where(lane == 2, 1.0 / den, jnp.where(lane == 3, e1 / den, 0.0))))
    o_ref[...] = out


def _router(x_f32, w_router_pad, n_experts, *, tm):
    n, d = x_f32.shape
    return pl.pallas_call(
        functools.partial(_router_kernel, n_experts=n_experts),
        grid=(n // tm,),
        in_specs=[pl.BlockSpec((tm, d), lambda t: (t, 0)),
                  pl.BlockSpec((d, LANE), lambda t: (0, 0))],
        out_specs=pl.BlockSpec((tm, LANE), lambda t: (t, 0)),
        out_shape=jax.ShapeDtypeStruct((n, LANE), F32),
        compiler_params=_cparams(("arbitrary",)),
        name="moe_router",
    )(x_f32, w_router_pad)


def _gather_kernel(src_ref, x_hbm, o_ref, buf_ref, sem, *, tr):
    def row_copy(r):
        return pltpu.make_async_copy(x_hbm.at[pl.ds(src_ref[r], 1)], buf_ref.at[pl.ds(r, 1)], sem)

    def issue(r, c):
        row_copy(r).start()
        return c

    def drain(r, c):
        row_copy(r).wait()
        return c

    lax.fori_loop(0, tr, issue, 0)
    lax.fori_loop(0, tr, drain, 0)
    o_ref[...] = buf_ref[...].astype(o_ref.dtype)


def _gather_rows(x_f32, src, *, tr):
    d = x_f32.shape[1]
    rows = src.shape[0]
    return pl.pallas_call(
        functools.partial(_gather_kernel, tr=tr),
        grid=(rows // tr,),
        in_specs=[pl.BlockSpec((tr,), lambda t: (t,), memory_space=pltpu.SMEM),
                  pl.BlockSpec(memory_space=pl.ANY)],
        out_specs=pl.BlockSpec((tr, d), lambda t: (t, 0)),
        out_shape=jax.ShapeDtypeStruct((rows, d), BF16),
        scratch_shapes=[pltpu.VMEM((tr, d), F32), pltpu.SemaphoreType.DMA(())],
        compiler_params=_cparams(("arbitrary",)),
        name="moe_gather",
    )(src, x_f32)


def _combine_kernel(s0_ref, s1_ref, y_hbm, x_ref, g_ref, b_ref, of_ref, ob_ref, buf0, buf1, sem, *, tc, alpha):
    def copies(r):
        return (pltpu.make_async_copy(y_hbm.at[pl.ds(s0_ref[r], 1)], buf0.at[pl.ds(r, 1)], sem),
                pltpu.make_async_copy(y_hbm.at[pl.ds(s1_ref[r], 1)], buf1.at[pl.ds(r, 1)], sem))

    def issue(r, c):
        a, b = copies(r)
        a.start()
        b.start()
        return c

    def drain(r, c):
        a, b = copies(r)
        a.wait()
        b.wait()
        return c

    lax.fori_loop(0, tc, issue, 0)
    lax.fori_loop(0, tc, drain, 0)
    y = _layer_norm(alpha * x_ref[...] + (buf0[...] + buf1[...]), g_ref[...], b_ref[...])
    of_ref[...] = y
    ob_ref[...] = y.astype(BF16)


def _moe_combine(ys, slot0, slot1, x_f32, ln_g, ln_b, alpha, *, tc):
    n, d = x_f32.shape
    return pl.pallas_call(
        functools.partial(_combine_kernel, tc=tc, alpha=alpha),
        grid=(n // tc,),
        in_specs=[pl.BlockSpec((tc,), lambda t: (t,), memory_space=pltpu.SMEM),
                  pl.BlockSpec((tc,), lambda t: (t,), memory_space=pltpu.SMEM),
                  pl.BlockSpec(memory_space=pl.ANY),
                  pl.BlockSpec((tc, d), lambda t: (t, 0)),
                  pl.BlockSpec((1, d), lambda t: (0, 0)),
                  pl.BlockSpec((1, d), lambda t: (0, 0))],
        out_specs=[pl.BlockSpec((tc, d), lambda t: (t, 0)), pl.BlockSpec((tc, d), lambda t: (t, 0))],
        out_shape=[jax.ShapeDtypeStruct((n, d), F32), jax.ShapeDtypeStruct((n, d), BF16)],
        scratch_shapes=[pltpu.VMEM((tc, d), F32), pltpu.VMEM((tc, d), F32), pltpu.SemaphoreType.DMA(())],
        compiler_params=_cparams(("arbitrary",)),
        name="moe_combine",
    )(slot0, slot1, ys, x_f32, ln_g.reshape(1, d), ln_b.reshape(1, d))


def _moe_plan(route, n_experts, tm):
    n = route.shape[0]
    e = jnp.concatenate([route[:, 0], route[:, 1]]).astype(jnp.int32)
    gate = jnp.concatenate([route[:, 2], route[:, 3]])
    token = jnp.concatenate([jnp.arange(n, dtype=jnp.int32)] * TOP_K)
    onehot = (e[:, None] == jnp.arange(n_experts, dtype=jnp.int32)[None, :]).astype(jnp.int32)
    csum = jnp.cumsum(onehot, axis=0)
    rank = jnp.take_along_axis(csum, e[:, None], axis=1)[:, 0] - 1
    counts = csum[-1]
    tiles_per = (counts + tm - 1) // tm
    tile_end = jnp.cumsum(tiles_per)
    row_start = (tile_end - tiles_per) * tm
    dest = row_start[e] + rank
    n_tiles = (TOP_K * n + n_experts * (tm - 1)) // tm
    rows = n_tiles * tm
    src = jnp.zeros((rows,), jnp.int32).at[dest].set(token)
    gate_rows = jnp.zeros((rows,), F32).at[dest].set(gate)
    tile_expert = jnp.minimum(
        jnp.searchsorted(tile_end, jnp.arange(n_tiles, dtype=jnp.int32), side="right"), n_experts - 1).astype(jnp.int32)
    local_tile = jnp.arange(n_tiles, dtype=jnp.int32) - (tile_end - tiles_per)[tile_expert]
    tile_rows = jnp.clip(counts[tile_expert] - local_tile * tm, 0, tm).astype(jnp.int32)
    return src, gate_rows.reshape(rows, 1), tile_expert, tile_rows, dest[:n], dest[n:]


def _cast_rows(total_rows, steps):
    r = 16
    while total_rows % r or total_rows // r > steps:
        r += 16
    return r


def _ffn_dense(x_f32, x_bf, w1, w3, w2, ln_g, ln_b, alpha, cfg, to_cast=()):
    n, d = x_bf.shape
    f = w1.shape[-1]
    tm, tf = cfg["ffn_tm"], cfg["ffn_tf"]
    te = jnp.zeros((n // tm,), jnp.int32)
    nu = jnp.full((n // tm,), tm, jnp.int32)
    steps = (f // tf) * (n // tm)
    casts = []
    for c, arr in enumerate(to_cast):
        flat = arr.reshape(-1, arr.shape[-1])
        rows = _cast_rows(flat.shape[0], steps // len(to_cast) + steps // 64)
        start = c * (steps - flat.shape[0] // rows) // max(len(to_cast) - 1, 1)
        casts.append((flat, rows, start))
    a, *copies = _matmul_wres(x_bf, [w1[None], w3[None]], te, nu, tm=tm, tn=tf, col0=0, n_cols=f,
                              out_dtypes=[BF16], epilogue=_epi_swiglu, name="ffn_gate_up", casts=casts)
    y, y_bf = _proj_resid_ln(a, w2, x_f32, ln_g, ln_b, alpha, tm=cfg["ln_tm"], tk=cfg["down_tk"], name="ffn_down_ln")
    return y, y_bf, [cp.reshape(arr.shape) for cp, arr in zip(copies, to_cast)]


def _ffn_moe(x_f32, x_bf, w_router_pad, n_experts, w1, w3, w2_f32, ln_g, ln_b, alpha, cfg):
    n, d = x_bf.shape
    f = w1.shape[-1]
    tm, tf = cfg["moe_tm"], cfg["ffn_tf"]
    route = _router(x_f32, w_router_pad, n_experts, tm=cfg["row_tm"])
    src, gate_rows, tile_expert, tile_rows, slot0, slot1 = _moe_plan(route, n_experts, tm)
    xs = _gather_rows(x_f32, src, tr=cfg["gather_tr"])
    steps = (f // tf) * (xs.shape[0] // tm)
    a, w2 = _matmul_wres(xs, [w1, w3], tile_expert, tile_rows, tm=tm, tn=tf, col0=0, n_cols=f,
                         out_dtypes=[BF16], epilogue=_epi_swiglu, name="moe_gate_up",
                         casts=[(w2_f32.reshape(n_experts * f, d), _cast_rows(n_experts * f, steps), 0)])
    w2 = w2.reshape(n_experts, f, d)
    ys, = _matmul_acc(a, w2, tile_expert, tile_rows, [gate_rows],
                      [pl.BlockSpec((tm, 1), lambda t, k, te, nu: (t, 0))],
                      tm=tm, tk=cfg["moe_down_tk"], out_dtypes=[F32], epilogue=_epi_row_scale, name="moe_down")
    return _moe_combine(ys, slot0, slot1, x_f32, ln_g, ln_b, alpha, tc=cfg["combine_tc"])


def _split_w_in(w_in):
    sizes = (W_A, W_A, W_A, H_A, W_B, W_B, W_B, QK_C, QK_C, W_C, W_C, GATE_RANK)
    offs = np.concatenate([[0], np.cumsum(sizes)])
    w_t = w_in.T
    part = lambda i: w_t[offs[i]:offs[i + 1]]
    qa, ka, va, fa, qb, kb, vb, qc, kc, vc, rc, ac = [part(i) for i in range(12)]
    main = jnp.concatenate([ka, va, kb, vb, qc, kc, vc, rc, qa, qb], axis=0).astype(BF16)
    d = w_in.shape[0]
    fa_t = jnp.concatenate([fa, jnp.zeros((16 - H_A, d), F32)], axis=0).astype(BF16)
    ac_t = jnp.concatenate([ac, jnp.zeros((LANE - GATE_RANK, d), F32)], axis=0).astype(BF16)
    return main, fa_t, ac_t


def _mixer(x_bf, dims, layer, caches, params, cfg):
    batch, seq, dec_batch, t_new = dims
    fk_all, fv_all, bk_all, bv_all, gla_all, c_flogf, c_bk, c_bv = caches
    w_main, w_fa_t, w_ac_pad, b_forget, rel_bias, w_gate_up, b_gate, gla_norm_g = params
    n = x_bf.shape[0]
    n_p = batch * seq
    past = fk_all.shape[1]
    n_band = bk_all.shape[1]
    assert seq % CHUNK == 0 and t_new == CHUNK and past % CHUNK == 0 and n_band % LANE == 0
    assert n_band == min(BAND_PAST, past) and n_p % t_new == 0

    tm, tn = cfg["proj_tm"], cfg["proj_tn"]
    te = jnp.zeros((n // tm,), jnp.int32)
    nu = jnp.full((n // tm,), tm, jnp.int32)
    zs_f32, zs = _matmul_wres(x_bf, [w_main[None]], te, nu, tm=tm, tn=tn, col0=0, n_cols=W_STATE,
                              out_dtypes=[F32, BF16], epilogue=_epi_store, name="proj_state", w_transposed=True)
    zr, = _matmul_wres(x_bf, [w_main[None]], te, nu, tm=tm, tn=tn, col0=W_STATE, n_cols=W_REST,
                       out_dtypes=[BF16], epilogue=_epi_store, name="proj_rest", w_transposed=True)

    b_col = jnp.zeros((16, 1), F32).at[:H_A, 0].set(b_forget)
    logf_t = _forget_logits(x_bf, w_fa_t, b_col, tm=cfg["row_tm"])
    w_up_pad = jnp.zeros((LANE, QK_C), F32).at[:GATE_RANK].set(w_gate_up).astype(BF16)
    g = _gla_gates(x_bf, w_ac_pad, w_up_pad, b_gate, tm=cfg["row_tm"])

    hp = cfg["fox_hp"]
    logf_bh = logf_t[:H_A, :n_p].reshape(H_A, batch, seq).transpose(1, 0, 2).reshape(batch * H_A, seq)
    c_p = _cumsum_last(logf_bh).reshape(batch * H_A // hp, hp, seq)
    o_cat = jnp.zeros((n, MIX_WIDTH), BF16)
    o_cat = _fox_prompt(o_cat, zr, zs, c_p, batch, seq, tq=cfg["fox_tq"], hp=hp)
    logf_s = logf_t[:H_A, n_p:].reshape(H_A, dec_batch, t_new).transpose(1, 0, 2)
    logf_all = jnp.concatenate([c_flogf.transpose(0, 2, 1), logf_s], axis=2)
    c_s = _cumsum_last(logf_all.reshape(dec_batch * H_A, past + t_new))
    hg = cfg["fox_hg"]
    c_s = c_s.reshape(dec_batch * (H_A // hg), hg, past + t_new)
    cache_b0 = layer * dec_batch
    o_cat = _fox_sample(o_cat, zr, zs, fk_all, fv_all, cache_b0, c_s, n_p // t_new, dec_batch, t_new, hg=hg)

    bias_p, bias_s = _band_bias(rel_bias, tq=cfg["band_tq"], t_new=t_new, n_past=n_band)
    o_cat = _band_prompt(o_cat, zr, zs, bias_p, batch, seq, tq=cfg["band_tq"])
    o_cat = _band_sample(o_cat, zr, zs, bk_all, bv_all, cache_b0, bias_s, n_p // t_new, dec_batch, t_new)

    s_zero = jnp.zeros((batch, H_C, DK_C, DV_C), F32)
    o_cat, s_p = _gla(o_cat, zr, g, s_zero, 0, gla_norm_g, 0, batch, seq // CHUNK, chunk=CHUNK)
    o_cat, s_s = _gla(o_cat, zr, g, gla_all, cache_b0, gla_norm_g, n_p // CHUNK, dec_batch, 1, chunk=CHUNK)

    ka, va = zs_f32[:, ST_KA:ST_KA + W_A], zs_f32[:, ST_VA:ST_VA + W_A]
    kb, vb = zs_f32[:, ST_KB:ST_KB + W_B], zs_f32[:, ST_VB:ST_VB + W_B]
    heads = lambda t, b, l, h: t.reshape(b, l, h, HEAD_DIM)
    nbp = min(BAND_PAST, seq)
    logf_p = logf_t[:H_A, :n_p].reshape(H_A, batch, seq).transpose(1, 2, 0)
    st_p = (heads(ka[:n_p], batch, seq, H_A), heads(va[:n_p], batch, seq, H_A), logf_p,
            heads(kb[:n_p], batch, seq, H_B)[:, seq - nbp:], heads(vb[:n_p], batch, seq, H_B)[:, seq - nbp:], s_p)
    kb_s, vb_s = heads(kb[n_p:], dec_batch, t_new, H_B), heads(vb[n_p:], dec_batch, t_new, H_B)
    st_s = (heads(ka[n_p:], dec_batch, t_new, H_A), heads(va[n_p:], dec_batch, t_new, H_A),
            logf_s.transpose(0, 2, 1),
            jnp.concatenate([c_bk, kb_s], axis=1)[:, t_new:], jnp.concatenate([c_bv, vb_s], axis=1)[:, t_new:], s_s)
    return o_cat, st_p, st_s


def _config(n, d, f, seq):
    return dict(
        proj_tm=_tile(n, 1024, 256), proj_tn=_tile(W_STATE, 1024, 512),
        row_tm=_tile(n, 512, 256),
        ffn_tm=_tile(n, 512, 256), ffn_tf=_tile(f, 512, 128),
        ln_tm=_tile(n, 512, 128), down_tk=_tile(f, 512, 128), out_tk=_tile(MIX_WIDTH, 512, 128),
        moe_down_tk=_tile(f, 1024, 128),
        moe_tm=512 if n >= 4096 else 128, gather_tr=256 if n >= 4096 else 128,
        combine_tc=_tile(n, 256, 128),
        fox_tq=_tile(seq, 512, 128), fox_hp=2, fox_hg=4, band_tq=_tile(seq, 512, 128),
    )


def kernel(x_prompt, x_sample, cache_fox_k, cache_fox_v, cache_fox_logf, cache_band_k, cache_band_v, state_gla,
           w_in, b_forget, rel_bias, w_gate_up, b_gate, gla_norm_g, w_out,
           ln1_g, ln1_b, ln2_g, ln2_b, ffn_w1, ffn_w3, ffn_w2,
           moe_router, moe_w1, moe_w3, moe_w2):
    batch, seq, d = x_prompt.shape
    dec_batch, t_new, _ = x_sample.shape
    depth = w_in.shape[0]
    f = ffn_w1.shape[-1]
    n_experts = moe_router.shape[-1]
    n_p, n_s = batch * seq, dec_batch * t_new
    n = n_p + n_s
    alpha = (2.0 * depth) ** 0.25
    cfg = _config(n, d, f, seq)
    dims = (batch, seq, dec_batch, t_new)

    x = jnp.concatenate([x_prompt.reshape(n_p, d), x_sample.reshape(n_s, d)], axis=0)
    x_bf = x.astype(BF16)
    sp_all, ss_all = [], []
    past, n_band = cache_fox_k.shape[2], cache_band_k.shape[2]
    head_major = lambda c: c.transpose(0, 1, 3, 2, 4).reshape(depth * dec_batch * H_A, past, HEAD_DIM)
    fk_all, fv_all = head_major(cache_fox_k), head_major(cache_fox_v)
    bk_all = cache_band_k.reshape(depth * dec_batch, n_band, W_B)
    bv_all = cache_band_v.reshape(depth * dec_batch, n_band, W_B)
    gla_all = state_gla.reshape(depth * dec_batch, H_C, DK_C, DV_C)
    for i in range(depth):
        w_main, w_fa_t, w_ac_pad = _split_w_in(w_in[i])
        caches = (fk_all, fv_all, bk_all, bv_all, gla_all, cache_fox_logf[i], cache_band_k[i], cache_band_v[i])
        params = (w_main, w_fa_t, w_ac_pad, b_forget[i], rel_bias[i], w_gate_up[i], b_gate[i], gla_norm_g[i])
        o_cat, st_p, st_s = _mixer(x_bf, dims, i, caches, params, cfg)
        sp_all.append(st_p)
        ss_all.append(st_s)
        w_o = w_out[i]
        w_o = jnp.concatenate([w_o[W_A + W_B:], w_o[:W_A], w_o[W_A:W_A + W_B]], axis=0).astype(BF16)
        x, x_bf = _proj_resid_ln(o_cat, w_o, x, ln1_g[i], ln1_b[i], alpha,
                                 tm=cfg["ln_tm"], tk=cfg["out_tk"], name="out_proj_ln")
        j = i // 2
        if i % 2 == 0:
            to_cast = (moe_w1[j], moe_w3[j]) if i + 1 < depth else ()
            x, x_bf, moe_gate_up_bf = _ffn_dense(x, x_bf, ffn_w1[j].astype(BF16), ffn_w3[j].astype(BF16),
                                                 ffn_w2[j].astype(BF16), ln2_g[i], ln2_b[i], alpha, cfg, to_cast)
        else:
            w_router_pad = jnp.zeros((d, LANE), F32).at[:, :n_experts].set(moe_router[j])
            x, x_bf = _ffn_moe(x, x_bf, w_router_pad, n_experts, *moe_gate_up_bf,
                               moe_w2[j], ln2_g[i], ln2_b[i], alpha, cfg)
    stack = lambda states, k: jnp.stack([s[k] for s in states], axis=0)
    y_p = x[:n_p].reshape(batch, seq, d)
    y_s = x[n_p:].reshape(dec_batch, t_new, d)
    return (y_p, y_s) + tuple(stack(sp_all, k) for k in range(6)) + tuple(stack(ss_all, k) for k in range(6))
```

```python
import functools
import math

import jax
import jax.numpy as jnp
import numpy as np
from jax import lax
from jax.experimental import pallas as pl
from jax.experimental.pallas import tpu as pltpu

CHUNK = 64
HEAD_DIM = 128
H_A = 12
H_B = 8
H_C = 6
DK_C = 128
DV_C = 256
W_A = H_A * HEAD_DIM
W_B = H_B * HEAD_DIM
QK_C = H_C * DK_C
W_C = H_C * DV_C
GATE_RANK = 16
GLA_TAU = 16.0
BAND_CHUNKS = 8
BAND_PAST = BAND_CHUNKS * CHUNK
MAX_REL = 128
TOP_K = 2
LN_EPS = 1e-5
RMS_EPS = 1e-6
ATTN_SCALE = HEAD_DIM ** -0.5

LANE = 128
NEG_INF = float("-inf")
LOG2E = math.log2(math.e)
F32 = jnp.float32
BF16 = jnp.bfloat16
VMEM_LIMIT = 56 * 1024 * 1024

ST_KA, ST_VA, ST_KB, ST_VB = 0, W_A, 2 * W_A, 2 * W_A + W_B
W_STATE = 2 * W_A + 2 * W_B
RS_QC, RS_KC, RS_VC, RS_RC, RS_QA, RS_QB = 0, QK_C, 2 * QK_C, 2 * QK_C + W_C, 2 * QK_C + 2 * W_C, 2 * QK_C + 2 * W_C + W_A
W_REST = RS_QB + W_B
OC_C, OC_A, OC_B = 0, W_C, W_C + W_A
MIX_WIDTH = W_A + W_B + W_C


def _tile(dim, pref, mult):
    if dim <= pref:
        return dim
    t = (pref // mult) * mult
    while t > mult and dim % t:
        t -= mult
    assert dim % t == 0, (dim, pref, mult)
    return t


def _cparams(sem):
    return pltpu.CompilerParams(dimension_semantics=sem, vmem_limit_bytes=VMEM_LIMIT)


def _carry_output(body):
    def wrapped(_, *refs):
        body(*refs)
    return wrapped


def _carried(o_cat, body):
    return _carry_output(body), [pl.BlockSpec(memory_space=pl.ANY)], [o_cat], {0: 0}


def _log_sigmoid(x):
    return jnp.minimum(x, 0.0) - jnp.log(1.0 + jnp.exp(-jnp.abs(x)))


def _silu(x):
    return x / (1.0 + jnp.exp(-x))


def _layer_norm(x, g, b):
    mu = jnp.mean(x, axis=-1, keepdims=True)
    xc = x - mu
    var = jnp.mean(xc * xc, axis=-1, keepdims=True)
    return xc * lax.rsqrt(var + LN_EPS) * g + b


NT_DIMS = (((1,), (1,)), ((), ()))


def _wres_kernel(te_ref, nu_ref, x_ref, *refs, n_w, n_out, epilogue, w_transposed, cast_spans):
    w_refs = refs[:n_w]
    out_refs = refs[n_w + len(cast_spans):][:n_out]
    t = pl.program_id(1)
    valid = nu_ref[t]
    half = x_ref.shape[0] // 2

    def compute(x, outs):
        if w_transposed:
            accs = [lax.dot_general(x, w[0], NT_DIMS, preferred_element_type=F32) for w in w_refs]
        else:
            accs = [jnp.dot(x, w[0], preferred_element_type=F32) for w in w_refs]
        epilogue(accs, outs)

    @pl.when(valid > half)
    def _():
        compute(x_ref[...], out_refs)

    @pl.when((valid > 0) & (valid <= half))
    def _():
        compute(x_ref[:half, :], [o.at[pl.ds(0, half)] for o in out_refs])
        for o in out_refs:
            o[half:, :] = jnp.zeros((o.shape[0] - half, o.shape[1]), o.dtype)

    @pl.when(valid == 0)
    def _():
        for o in out_refs:
            o[...] = jnp.zeros(o.shape, o.dtype)

    step = pl.program_id(0) * pl.num_programs(1) + t
    n_cast = len(cast_spans)
    for c, (start, n_blocks) in enumerate(cast_spans):
        src_ref, dst_ref = refs[n_w + c], refs[n_w + n_cast + n_out + c]

        @pl.when((step >= start) & (step < start + n_blocks))
        def _(src_ref=src_ref, dst_ref=dst_ref):
            dst_ref[...] = src_ref[...].astype(dst_ref.dtype)


def _matmul_wres(x, ws, tile_expert, tile_rows, *, tm, tn, col0, n_cols, out_dtypes, epilogue, name,
                 w_transposed=False, casts=()):
    rows, kdim = x.shape
    nt = rows // tm
    nj = n_cols // tn
    j0 = col0 // tn
    assert rows % tm == 0 and n_cols % tn == 0 and col0 % tn == 0
    if w_transposed:
        w_spec = pl.BlockSpec((1, tn, kdim), lambda j, t, te, nu: (te[t], j + j0, 0))
    else:
        w_spec = pl.BlockSpec((1, kdim, tn), lambda j, t, te, nu: (te[t], 0, j + j0))
    in_specs = [pl.BlockSpec((tm, kdim), lambda j, t, te, nu: (t, 0))] + [w_spec for _ in ws]
    out_specs = [pl.BlockSpec((tm, tn), lambda j, t, te, nu: (t, j)) for _ in out_dtypes]
    out_shape = [jax.ShapeDtypeStruct((rows, n_cols), dt) for dt in out_dtypes]
    args = [x, *ws]
    cast_spans = []
    for src, rows_per_step, start in casts:
        n_blocks = src.shape[0] // rows_per_step
        assert src.shape[0] % rows_per_step == 0 and start + n_blocks <= nj * nt
        cast_spans.append((start, n_blocks))

        def cast_index(j, t, te, nu, start=start, n_blocks=n_blocks):
            return jnp.clip(j * nt + t - start, 0, n_blocks - 1), 0

        in_specs.append(pl.BlockSpec((rows_per_step, src.shape[1]), cast_index))
        args.append(src)
    for (src, rows_per_step, _), spec in zip(casts, in_specs[1 + len(ws):]):
        out_specs.append(spec)
        out_shape.append(jax.ShapeDtypeStruct(src.shape, BF16))
    grid_spec = pltpu.PrefetchScalarGridSpec(
        num_scalar_prefetch=2, grid=(nj, nt), in_specs=in_specs, out_specs=out_specs)
    return pl.pallas_call(
        functools.partial(_wres_kernel, n_w=len(ws), n_out=len(out_dtypes), epilogue=epilogue,
                          w_transposed=w_transposed, cast_spans=tuple(cast_spans)),
        grid_spec=grid_spec,
        out_shape=out_shape,
        compiler_params=_cparams(("arbitrary", "arbitrary")),
        name=name,
    )(tile_expert, tile_rows, *args)


def _epi_store(accs, outs):
    for o in outs:
        o[...] = accs[0].astype(o.dtype)


def _epi_swiglu(accs, outs):
    outs[0][...] = (_silu(accs[0]) * accs[1]).astype(outs[0].dtype)


ROW_SLAB = 32
SLABS_PER_TRIP = 2


def _acc_kernel(te_ref, nu_ref, a_ref, w_ref, *refs, n_extra, epilogue):
    extra = refs[:n_extra]
    outs = refs[n_extra:]
    acc_ref = outs[0]
    t = pl.program_id(0)
    k = pl.program_id(1)
    nk = pl.num_programs(1)
    valid = nu_ref[t]
    half = a_ref.shape[0] // 2
    used = valid > 0
    full = valid > half
    part = used & jnp.logical_not(full)

    @pl.when(full & (k == 0))
    def _():
        acc_ref[...] = jnp.dot(a_ref[...], w_ref[0], preferred_element_type=F32)

    @pl.when(full & (k > 0))
    def _():
        acc_ref[...] += jnp.dot(a_ref[...], w_ref[0], preferred_element_type=F32)

    @pl.when(part & (k == 0))
    def _():
        acc_ref[:half, :] = jnp.dot(a_ref[:half, :], w_ref[0], preferred_element_type=F32)
        acc_ref[half:, :] = jnp.zeros((acc_ref.shape[0] - half, acc_ref.shape[1]), F32)

    @pl.when(part & (k > 0))
    def _():
        acc_ref[:half, :] += jnp.dot(a_ref[:half, :], w_ref[0], preferred_element_type=F32)

    @pl.when(used & (k == nk - 1))
    def _():
        def slabs(i, c):
            base = i * (SLABS_PER_TRIP * ROW_SLAB)
            rows = [pl.ds(pl.multiple_of(base + s * ROW_SLAB, ROW_SLAB), ROW_SLAB) for s in range(SLABS_PER_TRIP)]
            epilogue(rows, extra, outs)
            return c

        lax.fori_loop(0, acc_ref.shape[0] // (SLABS_PER_TRIP * ROW_SLAB), slabs, 0)

    @pl.when(jnp.logical_not(used) & (k == nk - 1))
    def _():
        for o in outs:
            o[...] = jnp.zeros(o.shape, o.dtype)


def _matmul_acc(a, w, tile_expert, tile_rows, extra, extra_specs, *, tm, tk, out_dtypes, epilogue, name):
    rows, kdim = a.shape
    n_out = w.shape[2]
    nt = rows // tm
    nk = kdim // tk
    assert rows % tm == 0 and kdim % tk == 0 and tm % ROW_SLAB == 0 and out_dtypes[0] == F32
    grid_spec = pltpu.PrefetchScalarGridSpec(
        num_scalar_prefetch=2,
        grid=(nt, nk),
        in_specs=[pl.BlockSpec((tm, tk), lambda t, k, te, nu: (t, k)),
                  pl.BlockSpec((1, tk, n_out), lambda t, k, te, nu: (te[t], k, 0))] + extra_specs,
        out_specs=[pl.BlockSpec((tm, n_out), lambda t, k, te, nu: (t, 0)) for _ in out_dtypes],
    )
    return pl.pallas_call(
        functools.partial(_acc_kernel, n_extra=len(extra), epilogue=epilogue),
        grid_spec=grid_spec,
        out_shape=[jax.ShapeDtypeStruct((rows, n_out), dt) for dt in out_dtypes],
        compiler_params=_cparams(("arbitrary", "arbitrary")),
        name=name,
    )(tile_expert, tile_rows, a, w, *extra)


def _epi_resid_ln(row_slabs, extra, outs, *, alpha):
    x_ref, g_ref, b_ref = extra
    hs = [alpha * x_ref[rows, :] + outs[0][rows, :] for rows in row_slabs]
    ys = [_layer_norm(h, g_ref[...], b_ref[...]) for h in hs]
    for rows, y in zip(row_slabs, ys):
        outs[0][rows, :] = y
        outs[1][rows, :] = y.astype(BF16)


def _epi_row_scale(row_slabs, extra, outs):
    ys = [outs[0][rows, :] * extra[0][rows, :] for rows in row_slabs]
    for rows, y in zip(row_slabs, ys):
        outs[0][rows, :] = y


def _proj_resid_ln(a, w, x_f32, ln_g, ln_b, alpha, *, tm, tk, name):
    rows = a.shape[0]
    d = w.shape[-1]
    te = jnp.zeros((rows // tm,), jnp.int32)
    nu = jnp.full((rows // tm,), tm, jnp.int32)
    extra_specs = [pl.BlockSpec((tm, d), lambda t, k, te, nu: (t, 0)),
                   pl.BlockSpec((1, d), lambda t, k, te, nu: (0, 0)),
                   pl.BlockSpec((1, d), lambda t, k, te, nu: (0, 0))]
    return _matmul_acc(a, w.reshape((1,) + w.shape[-2:]), te, nu,
                       [x_f32, ln_g.reshape(1, d), ln_b.reshape(1, d)], extra_specs,
                       tm=tm, tk=tk, out_dtypes=[F32, BF16],
                       epilogue=functools.partial(_epi_resid_ln, alpha=alpha), name=name)


def _forget_kernel(wt_ref, b_ref, x_ref, o_ref):
    fa = lax.dot_general(wt_ref[...], x_ref[...], (((1,), (1,)), ((), ())), preferred_element_type=F32)
    o_ref[...] = _log_sigmoid(fa + b_ref[...])


def _forget_logits(x_bf, w_fa_t, b_col, *, tm):
    n, d = x_bf.shape
    hp = w_fa_t.shape[0]
    return pl.pallas_call(
        _forget_kernel,
        grid=(n // tm,),
        in_specs=[pl.BlockSpec((hp, d), lambda t: (0, 0)),
                  pl.BlockSpec((hp, 1), lambda t: (0, 0)),
                  pl.BlockSpec((tm, d), lambda t: (t, 0))],
        out_specs=pl.BlockSpec((hp, tm), lambda t: (0, t)),
        out_shape=jax.ShapeDtypeStruct((hp, n), F32),
        compiler_params=_cparams(("arbitrary",)),
        name="forget_logits",
    )(w_fa_t, b_col, x_bf)


def _gate_kernel(x_ref, wa_ref, wu_ref, b_ref, o_ref):
    ac = lax.dot_general(x_ref[...], wa_ref[...], NT_DIMS, preferred_element_type=F32)
    pre = jnp.dot(ac.astype(BF16), wu_ref[...], preferred_element_type=F32) + b_ref[...]
    o_ref[...] = _log_sigmoid(pre) * (1.0 / GLA_TAU)


def _gla_gates(x_bf, w_ac_pad, w_up_pad, b_gate, *, tm):
    n, d = x_bf.shape
    return pl.pallas_call(
        _gate_kernel,
        grid=(n // tm,),
        in_specs=[pl.BlockSpec((tm, d), lambda t: (t, 0)),
                  pl.BlockSpec((LANE, d), lambda t: (0, 0)),
                  pl.BlockSpec((LANE, QK_C), lambda t: (0, 0)),
                  pl.BlockSpec((1, QK_C), lambda t: (0, 0))],
        out_specs=pl.BlockSpec((tm, QK_C), lambda t: (t, 0)),
        out_shape=jax.ShapeDtypeStruct((n, QK_C), F32),
        compiler_params=_cparams(("arbitrary",)),
        name="gla_gates",
    )(x_bf, w_ac_pad, w_up_pad, b_gate.reshape(1, QK_C))


def _cumsum_kernel(x_ref, o_ref, *, t_len):
    rows = x_ref.shape[0]
    ii = lax.broadcasted_iota(jnp.int32, (LANE, LANE), 0)
    jj = lax.broadcasted_iota(jnp.int32, (LANE, LANE), 1)
    upper = (ii <= jj).astype(F32)
    carry = jnp.zeros((rows, 1), F32)
    for c0 in range(0, t_len, LANE):
        w = min(LANE, t_len - c0)
        blk = x_ref[:, c0:c0 + w]
        cs = jnp.dot(blk, upper[:w, :w], preferred_element_type=F32, precision=lax.Precision.HIGHEST) + carry
        o_ref[:, c0:c0 + w] = cs
        carry = cs[:, w - 1:w]


def _cumsum_last(x):
    rows, t_len = x.shape
    rb = 8
    assert rows % rb == 0
    return pl.pallas_call(
        functools.partial(_cumsum_kernel, t_len=t_len),
        grid=(rows // rb,),
        in_specs=[pl.BlockSpec((rb, t_len), lambda r: (r, 0))],
        out_specs=pl.BlockSpec((rb, t_len), lambda r: (r, 0)),
        out_shape=jax.ShapeDtypeStruct((rows, t_len), F32),
        compiler_params=_cparams(("arbitrary",)),
        name="cumsum_time",
    )(x)


def _fox_prompt_kernel(q_ref, k_ref, v_ref, c_ref, o_ref, *, tq, hp):
    qi = pl.program_id(2)
    q0 = pl.multiple_of(qi * tq, tq)
    heads = [slice(h * HEAD_DIM, (h + 1) * HEAD_DIM) for h in range(hp)]
    qs = [q_ref[:, sl] for sl in heads]
    c_first = [c_ref[0, h:h + 1, pl.ds(q0, LANE)][:, :1] for h in range(hp)]

    def step(k0, carry, masked):
        out = []
        for h, sl in enumerate(heads):
            m, l, acc = carry[h]
            kb = k_ref[pl.ds(k0, tq), sl]
            vb = v_ref[pl.ds(k0, tq), sl]
            s = lax.dot_general(qs[h], kb, NT_DIMS, preferred_element_type=F32) * (ATTN_SCALE * LOG2E)
            s = s + (c_first[h] - c_ref[0, h:h + 1, pl.ds(k0, tq)]) * LOG2E
            if masked:
                ii = lax.broadcasted_iota(jnp.int32, (tq, tq), 0)
                jj = lax.broadcasted_iota(jnp.int32, (tq, tq), 1)
                s = jnp.where(jj <= ii, s, NEG_INF)
            m_new = jnp.maximum(m, jnp.max(s, axis=1, keepdims=True))
            a = jnp.exp2(m - m_new)
            p = jnp.exp2(s - m_new)
            l = a * l + jnp.sum(p, axis=1, keepdims=True)
            acc = a * acc + jnp.dot(p.astype(BF16), vb, preferred_element_type=F32)
            out.append((m_new, l, acc))
        return tuple(out)

    init = tuple((jnp.full((tq, 1), NEG_INF, F32), jnp.zeros((tq, 1), F32), jnp.zeros((tq, HEAD_DIM), F32))
                 for _ in heads)
    carry = lax.fori_loop(0, qi, lambda kj, c: step(pl.multiple_of(kj * tq, tq), c, False), init)
    carry = step(q0, carry, True)
    for (m, l, acc), sl in zip(carry, heads):
        o_ref[:, sl] = (acc / l).astype(o_ref.dtype)


def _fox_prompt(o_cat, zq, zkv, c_rows, batch, seq, *, tq, hp):
    nq = seq // tq
    wg = hp * HEAD_DIM
    ng = H_A // hp
    q_col, k_col, v_col, o_col = RS_QA // wg, ST_KA // wg, ST_VA // wg, OC_A // wg
    body, lead_specs, lead_args, aliases = _carried(o_cat, functools.partial(_fox_prompt_kernel, tq=tq, hp=hp))
    return pl.pallas_call(
        body,
        grid=(batch, ng, nq),
        in_specs=lead_specs + [
            pl.BlockSpec((tq, wg), lambda b, g, i: (b * nq + i, q_col + g)),
            pl.BlockSpec((seq, wg), lambda b, g, i: (b, k_col + g)),
            pl.BlockSpec((seq, wg), lambda b, g, i: (b, v_col + g)),
            pl.BlockSpec((1, hp, seq), lambda b, g, i: (b * ng + g, 0, 0))],
        out_specs=pl.BlockSpec((tq, wg), lambda b, g, i: (b * nq + i, o_col + g)),
        out_shape=jax.ShapeDtypeStruct((zq.shape[0], MIX_WIDTH), BF16),
        input_output_aliases=aliases,
        compiler_params=_cparams(("arbitrary", "arbitrary", "arbitrary")),
        name="fox_prompt",
    )(*lead_args, zq, zkv, zkv, c_rows)


def _fox_sample_kernel(q_ref, kn_ref, vn_ref, ck_ref, cv_ref, c_ref, o_ref, *, hg, past, t_new):
    ii = lax.broadcasted_iota(jnp.int32, (t_new, t_new), 0)
    jj = lax.broadcasted_iota(jnp.int32, (t_new, t_new), 1)
    causal = jj <= ii
    for hh in range(hg):
        sl = slice(hh * HEAD_DIM, (hh + 1) * HEAD_DIM)
        q = q_ref[:, sl]
        kc = ck_ref[hh].astype(BF16)
        vc = cv_ref[hh].astype(BF16)
        kn = kn_ref[:, sl]
        vn = vn_ref[:, sl]
        c_row = c_ref[0, hh:hh + 1, :]
        c_first = c_row[:, past:past + 1]
        s_c = lax.dot_general(q, kc, (((1,), (1,)), ((), ())), preferred_element_type=F32) * ATTN_SCALE
        s_c = s_c + (c_first - c_row[:, :past])
        s_n = lax.dot_general(q, kn, (((1,), (1,)), ((), ())), preferred_element_type=F32) * ATTN_SCALE
        s_n = jnp.where(causal, s_n + (c_first - c_row[:, past:]), NEG_INF)
        m = jnp.maximum(jnp.max(s_c, axis=1, keepdims=True), jnp.max(s_n, axis=1, keepdims=True))
        p_c = jnp.exp(s_c - m)
        p_n = jnp.exp(s_n - m)
        l = jnp.sum(p_c, axis=1, keepdims=True) + jnp.sum(p_n, axis=1, keepdims=True)
        o = jnp.dot(p_c.astype(BF16), vc, preferred_element_type=F32)
        o = o + jnp.dot(p_n.astype(BF16), vn, preferred_element_type=F32)
        o_ref[:, sl] = (o / l).astype(o_ref.dtype)


def _fox_sample(o_cat, zq, zkv, cache_k, cache_v, cache_b0, c_all, row_blk0, dec_batch, t_new, *, hg):
    past = cache_k.shape[1]
    ng = H_A // hg
    wg = hg * HEAD_DIM
    q_col = RS_QA // wg
    k_col = ST_KA // wg
    v_col = ST_VA // wg
    o_col = OC_A // wg
    body, lead_specs, lead_args, aliases = _carried(
        o_cat, functools.partial(_fox_sample_kernel, hg=hg, past=past, t_new=t_new))
    return pl.pallas_call(
        body,
        grid=(dec_batch, ng),
        in_specs=lead_specs + [
            pl.BlockSpec((t_new, wg), lambda b, g: (row_blk0 + b, q_col + g)),
            pl.BlockSpec((t_new, wg), lambda b, g: (row_blk0 + b, k_col + g)),
            pl.BlockSpec((t_new, wg), lambda b, g: (row_blk0 + b, v_col + g)),
            pl.BlockSpec((hg, past, HEAD_DIM), lambda b, g: ((cache_b0 + b) * ng + g, 0, 0)),
            pl.BlockSpec((hg, past, HEAD_DIM), lambda b, g: ((cache_b0 + b) * ng + g, 0, 0)),
            pl.BlockSpec((1, hg, past + t_new), lambda b, g: (b * ng + g, 0, 0))],
        out_specs=pl.BlockSpec((t_new, wg), lambda b, g: (row_blk0 + b, o_col + g)),
        out_shape=jax.ShapeDtypeStruct((zq.shape[0], MIX_WIDTH), BF16),
        input_output_aliases=aliases,
        compiler_params=_cparams(("arbitrary", "arbitrary")),
        name="fox_sample",
    )(*lead_args, zq, zkv, zkv, cache_k, cache_v, c_all)


def _band_bias_kernel(rb_ref, op_ref, os_ref, *, tq, t_new, n_past):
    h = pl.program_id(0)
    ii = lax.broadcasted_iota(jnp.int32, (LANE, LANE), 0)
    jj = lax.broadcasted_iota(jnp.int32, (LANE, LANE), 1)
    idx_same = jj - ii + MAX_REL
    idx_prev = jnp.maximum(jj - ii - LANE, -MAX_REL) + MAX_REL

    def body(r, carry):
        t_same, t_prev = carry
        val = rb_ref[r * H_B + h]
        return jnp.where(idx_same == r, val, t_same), jnp.where(idx_prev == r, val, t_prev)

    zeros = jnp.zeros((LANE, LANE), F32)
    t_same, t_prev = lax.fori_loop(0, 2 * MAX_REL + 1, body, (zeros, zeros))
    far = jnp.full((LANE, LANE), rb_ref[h], F32)
    neg = jnp.full((LANE, LANE), NEG_INF, F32)

    def block(rel_blk):
        return t_same if rel_blk == 0 else t_prev if rel_blk == -1 else far

    for bi in range(tq // LANE):
        for bj in range(2 * tq // LANE):
            koff = bj * LANE - tq
            rel_blk = koff // LANE - bi
            qc = (bi * LANE + ii) // CHUNK
            kc = (koff + jj + tq) // CHUNK - tq // CHUNK
            vis = (kc <= qc) & (kc >= qc - BAND_CHUNKS)
            tile = neg if rel_blk > 0 else jnp.where(vis, block(rel_blk), NEG_INF)
            op_ref[0, bi * LANE:(bi + 1) * LANE, bj * LANE:(bj + 1) * LANE] = tile

    for j0 in range(0, n_past + t_new, LANE):
        w = min(LANE, n_past + t_new - j0)
        rel_blk = (j0 - n_past) // LANE
        qc = ii // CHUNK
        kc = (j0 + jj) // CHUNK - n_past // CHUNK
        vis = (kc <= qc) & (kc >= qc - BAND_CHUNKS)
        tile = jnp.where(vis, block(rel_blk), NEG_INF)
        os_ref[0, :, j0:j0 + w] = tile[:t_new, :w]


def _band_bias(rel_bias, *, tq, t_new, n_past):
    assert tq % LANE == 0 and n_past % LANE == 0 and t_new <= LANE and LANE % CHUNK == 0
    assert MAX_REL == LANE
    return pl.pallas_call(
        functools.partial(_band_bias_kernel, tq=tq, t_new=t_new, n_past=n_past),
        grid=(H_B,),
        in_specs=[pl.BlockSpec(memory_space=pltpu.SMEM)],
        out_specs=[pl.BlockSpec((1, tq, 2 * tq), lambda h: (h, 0, 0)),
                   pl.BlockSpec((1, t_new, n_past + t_new), lambda h: (h, 0, 0))],
        out_shape=[jax.ShapeDtypeStruct((H_B, tq, 2 * tq), F32),
                   jax.ShapeDtypeStruct((H_B, t_new, n_past + t_new), F32)],
        compiler_params=_cparams(("arbitrary",)),
        name="band_bias",
    )(rel_bias.reshape(-1))


def _band_prompt_kernel(q_ref, k_ref, v_ref, bias_ref, o_ref, *, tq):
    t = pl.program_id(2)
    q = q_ref[...]
    lo = pl.multiple_of(jnp.maximum(t - 1, 0) * tq, tq)
    hi = pl.multiple_of(t * tq, tq)
    nt = (((1,), (1,)), ((), ()))
    s_l = lax.dot_general(q, k_ref[pl.ds(lo, tq), :], nt, preferred_element_type=F32) * ATTN_SCALE
    s_l = jnp.where(t > 0, s_l + bias_ref[0, :, :tq], NEG_INF)
    s_r = lax.dot_general(q, k_ref[pl.ds(hi, tq), :], nt, preferred_element_type=F32) * ATTN_SCALE
    s_r = s_r + bias_ref[0, :, tq:]
    m = jnp.maximum(jnp.max(s_l, axis=1, keepdims=True), jnp.max(s_r, axis=1, keepdims=True))
    p_l = jnp.exp(s_l - m)
    p_r = jnp.exp(s_r - m)
    l = jnp.sum(p_l, axis=1, keepdims=True) + jnp.sum(p_r, axis=1, keepdims=True)
    o = jnp.dot(p_l.astype(BF16), v_ref[pl.ds(lo, tq), :], preferred_element_type=F32)
    o = o + jnp.dot(p_r.astype(BF16), v_ref[pl.ds(hi, tq), :], preferred_element_type=F32)
    o_ref[...] = (o / l).astype(o_ref.dtype)


def _band_prompt(o_cat, zq, zkv, bias_p, batch, seq, *, tq):
    nq = seq // tq
    q_col = RS_QB // HEAD_DIM
    k_col = ST_KB // HEAD_DIM
    v_col = ST_VB // HEAD_DIM
    o_col = OC_B // HEAD_DIM
    body, lead_specs, lead_args, aliases = _carried(o_cat, functools.partial(_band_prompt_kernel, tq=tq))
    return pl.pallas_call(
        body,
        grid=(H_B, batch, nq),
        in_specs=lead_specs + [
            pl.BlockSpec((tq, HEAD_DIM), lambda h, b, i: (b * nq + i, q_col + h)),
            pl.BlockSpec((seq, HEAD_DIM), lambda h, b, i: (b, k_col + h)),
            pl.BlockSpec((seq, HEAD_DIM), lambda h, b, i: (b, v_col + h)),
            pl.BlockSpec((1, tq, 2 * tq), lambda h, b, i: (h, 0, 0))],
        out_specs=pl.BlockSpec((tq, HEAD_DIM), lambda h, b, i: (b * nq + i, o_col + h)),
        out_shape=jax.ShapeDtypeStruct((zq.shape[0], MIX_WIDTH), BF16),
        input_output_aliases=aliases,
        compiler_params=_cparams(("arbitrary", "arbitrary", "arbitrary")),
        name="band_prompt",
    )(*lead_args, zq, zkv, zkv, bias_p)


def _band_sample_kernel(q_ref, kn_ref, vn_ref, ck_ref, cv_ref, bias_ref, o_ref, *, n_past):
    nt = (((1,), (1,)), ((), ()))
    for h in range(H_B):
        sl = slice(h * HEAD_DIM, (h + 1) * HEAD_DIM)
        q = q_ref[:, sl]
        kc = ck_ref[0, :, sl].astype(BF16)
        vc = cv_ref[0, :, sl].astype(BF16)
        s_c = lax.dot_general(q, kc, nt, preferred_element_type=F32) * ATTN_SCALE + bias_ref[h, :, :n_past]
        s_n = lax.dot_general(q, kn_ref[:, sl], nt, preferred_element_type=F32) * ATTN_SCALE + bias_ref[h, :, n_past:]
        m = jnp.maximum(jnp.max(s_c, axis=1, keepdims=True), jnp.max(s_n, axis=1, keepdims=True))
        p_c = jnp.exp(s_c - m)
        p_n = jnp.exp(s_n - m)
        l = jnp.sum(p_c, axis=1, keepdims=True) + jnp.sum(p_n, axis=1, keepdims=True)
        o = jnp.dot(p_c.astype(BF16), vc, preferred_element_type=F32)
        o = o + jnp.dot(p_n.astype(BF16), vn_ref[:, sl], preferred_element_type=F32)
        o_ref[:, sl] = (o / l).astype(o_ref.dtype)


def _band_sample(o_cat, zq, zkv, cache_k, cache_v, cache_b0, bias_s, row_blk0, dec_batch, t_new):
    n_past = cache_k.shape[1]
    q_col = RS_QB // W_B
    k_col = ST_KB // W_B
    v_col = ST_VB // W_B
    o_col = OC_B // W_B
    body, lead_specs, lead_args, aliases = _carried(o_cat, functools.partial(_band_sample_kernel, n_past=n_past))
    return pl.pallas_call(
        body,
        grid=(dec_batch,),
        in_specs=lead_specs + [
            pl.BlockSpec((t_new, W_B), lambda b: (row_blk0 + b, q_col)),
            pl.BlockSpec((t_new, W_B), lambda b: (row_blk0 + b, k_col)),
            pl.BlockSpec((t_new, W_B), lambda b: (row_blk0 + b, v_col)),
            pl.BlockSpec((1, n_past, W_B), lambda b: (cache_b0 + b, 0, 0)),
            pl.BlockSpec((1, n_past, W_B), lambda b: (cache_b0 + b, 0, 0)),
            pl.BlockSpec((H_B, t_new, n_past + t_new), lambda b: (0, 0, 0))],
        out_specs=pl.BlockSpec((t_new, W_B), lambda b: (row_blk0 + b, o_col)),
        out_shape=jax.ShapeDtypeStruct((zq.shape[0], MIX_WIDTH), BF16),
        input_output_aliases=aliases,
        compiler_params=_cparams(("arbitrary",)),
        name="band_sample",
    )(*lead_args, zq, zkv, zkv, cache_k, cache_v, bias_s)


GLA_EXP_CLAMP = 80.0


def _gla_kernel(q_ref, k_ref, v_ref, r_ref, g_ref, s0_ref, gn_ref, o_ref, so_ref, st_ref, *, chunk):
    n = pl.program_id(1)
    nc = pl.num_programs(1)

    @pl.when(n == 0)
    def _():
        for h in range(H_C):
            st_ref[h] = s0_ref[0, h].T

    ii = lax.broadcasted_iota(jnp.int32, (chunk, chunk), 0)
    jj = lax.broadcasted_iota(jnp.int32, (chunk, chunk), 1)
    tril = jj <= ii
    g_cum = jnp.dot(tril.astype(F32), g_ref[...], preferred_element_type=F32, precision=lax.Precision.HIGHEST)
    nt = (((1,), (1,)), ((), ()))
    tn = (((0,), (0,)), ((), ()))
    for h in range(H_C):
        ks = slice(h * DK_C, (h + 1) * DK_C)
        vs = slice(h * DV_C, (h + 1) * DV_C)
        gc = g_cum[:, ks]
        g_last = gc[chunk - 1:chunk, :]
        qf = q_ref[:, ks].astype(F32) * (DK_C ** -0.5)
        kf = k_ref[:, ks].astype(F32)
        vh = v_ref[:, vs]
        q_dec = (qf * jnp.exp(gc)).astype(BF16)
        k_inv = (kf * jnp.exp(jnp.minimum(-gc, GLA_EXP_CLAMP))).astype(BF16)
        k_dec = (kf * jnp.exp(g_last - gc)).astype(BF16)
        s_t = st_ref[h]
        o = lax.dot_general(q_dec, s_t.astype(BF16), nt, preferred_element_type=F32)
        a = lax.dot_general(q_dec, k_inv, nt, preferred_element_type=F32)
        a = jnp.where(tril, a, 0.0)
        o = o + jnp.dot(a.astype(BF16), vh, preferred_element_type=F32)
        st_ref[h] = s_t * jnp.exp(g_last) + lax.dot_general(vh, k_dec, tn, preferred_element_type=F32)
        on = o * lax.rsqrt(jnp.mean(o * o, axis=-1, keepdims=True) + RMS_EPS) * gn_ref[...]
        o_ref[:, vs] = (on * _silu(r_ref[:, vs].astype(F32))).astype(o_ref.dtype)

    @pl.when(n == nc - 1)
    def _():
        for h in range(H_C):
            so_ref[0, h] = st_ref[h].T


def _gla(o_cat, zq, g, s0, s0_b0, gnorm, row_blk0, n_seq, n_chunks, *, chunk):
    qc, kc, vc, rc = RS_QC // QK_C, RS_KC // QK_C, RS_VC // W_C, RS_RC // W_C
    o_col = OC_C // W_C
    row = lambda b, n: row_blk0 + b * n_chunks + n
    body, lead_specs, lead_args, aliases = _carried(o_cat, functools.partial(_gla_kernel, chunk=chunk))
    return pl.pallas_call(
        body,
        grid=(n_seq, n_chunks),
        in_specs=lead_specs + [
            pl.BlockSpec((chunk, QK_C), lambda b, n: (row(b, n), qc)),
            pl.BlockSpec((chunk, QK_C), lambda b, n: (row(b, n), kc)),
            pl.BlockSpec((chunk, W_C), lambda b, n: (row(b, n), vc)),
            pl.BlockSpec((chunk, W_C), lambda b, n: (row(b, n), rc)),
            pl.BlockSpec((chunk, QK_C), lambda b, n: (row(b, n), 0)),
            pl.BlockSpec((1, H_C, DK_C, DV_C), lambda b, n: (s0_b0 + b, 0, 0, 0)),
            pl.BlockSpec((1, DV_C), lambda b, n: (0, 0))],
        out_specs=[pl.BlockSpec((chunk, W_C), lambda b, n: (row(b, n), o_col)),
                   pl.BlockSpec((1, H_C, DK_C, DV_C), lambda b, n: (b, 0, 0, 0))],
        out_shape=[jax.ShapeDtypeStruct((zq.shape[0], MIX_WIDTH), BF16),
                   jax.ShapeDtypeStruct((n_seq, H_C, DK_C, DV_C), F32)],
        input_output_aliases=aliases,
        scratch_shapes=[pltpu.VMEM((H_C, DV_C, DK_C), F32)],
        compiler_params=_cparams(("arbitrary", "arbitrary")),
        name="gla",
    )(*lead_args, zq, zq, zq, zq, g, s0, gnorm.reshape(1, DV_C))


def _router_kernel(x_ref, w_ref, o_ref, *, n_experts):
    logits = jnp.dot(x_ref[...], w_ref[...], preferred_element_type=F32, precision=lax.Precision.HIGHEST)
    lane = lax.broadcasted_iota(jnp.int32, logits.shape, 1)
    lg = jnp.where(lane < n_experts, logits, NEG_INF)
    m0 = jnp.max(lg, axis=1, keepdims=True)
    i0 = jnp.min(jnp.where(lg == m0, lane, LANE), axis=1, keepdims=True)
    lg1 = jnp.where(lane == i0, NEG_INF, lg)
    m1 = jnp.max(lg1, axis=1, keepdims=True)
    i1 = jnp.min(jnp.where(lg1 == m1, lane, LANE), axis=1, keepdims=True)
    e1 = jnp.exp(m1 - m0)
    den = 1.0 + e1
    out = jnp.where(lane == 0, i0.astype(F32),
                    jnp.where(lane == 1, i1.astype(F32),
                              jnp.where(lane == 2, 1.0 / den, jnp.where(lane == 3, e1 / den, 0.0))))
    o_ref[...] = out


def _router(x_f32, w_router_pad, n_experts, *, tm):
    n, d = x_f32.shape
    return pl.pallas_call(
        functools.partial(_router_kernel, n_experts=n_experts),
        grid=(n // tm,),
        in_specs=[pl.BlockSpec((tm, d), lambda t: (t, 0)),
                  pl.BlockSpec((d, LANE), lambda t: (0, 0))],
        out_specs=pl.BlockSpec((tm, LANE), lambda t: (t, 0)),
        out_shape=jax.ShapeDtypeStruct((n, LANE), F32),
        compiler_params=_cparams(("arbitrary",)),
        name="moe_router",
    )(x_f32, w_router_pad)


def _gather_kernel(src_ref, x_hbm, o_ref, buf_ref, sem, *, tr):
    def row_copy(r):
        return pltpu.make_async_copy(x_hbm.at[pl.ds(src_ref[r], 1)], buf_ref.at[pl.ds(r, 1)], sem)

    def issue(r, c):
        row_copy(r).start()
        return c

    def drain(r, c):
        row_copy(r).wait()
        return c

    lax.fori_loop(0, tr, issue, 0)
    lax.fori_loop(0, tr, drain, 0)
    o_ref[...] = buf_ref[...].astype(o_ref.dtype)


def _gather_rows(x_f32, src, *, tr):
    d = x_f32.shape[1]
    rows = src.shape[0]
    return pl.pallas_call(
        functools.partial(_gather_kernel, tr=tr),
        grid=(rows // tr,),
        in_specs=[pl.BlockSpec((tr,), lambda t: (t,), memory_space=pltpu.SMEM),
                  pl.BlockSpec(memory_space=pl.ANY)],
        out_specs=pl.BlockSpec((tr, d), lambda t: (t, 0)),
        out_shape=jax.ShapeDtypeStruct((rows, d), BF16),
        scratch_shapes=[pltpu.VMEM((tr, d), F32), pltpu.SemaphoreType.DMA(())],
        compiler_params=_cparams(("arbitrary",)),
        name="moe_gather",
    )(src, x_f32)


def _combine_kernel(s0_ref, s1_ref, y_hbm, x_ref, g_ref, b_ref, of_ref, ob_ref, buf0, buf1, sem, *, tc, alpha):
    def copies(r):
        return (pltpu.make_async_copy(y_hbm.at[pl.ds(s0_ref[r], 1)], buf0.at[pl.ds(r, 1)], sem),
                pltpu.make_async_copy(y_hbm.at[pl.ds(s1_ref[r], 1)], buf1.at[pl.ds(r, 1)], sem))

    def issue(r, c):
        a, b = copies(r)
        a.start()
        b.start()
        return c

    def drain(r, c):
        a, b = copies(r)
        a.wait()
        b.wait()
        return c

    lax.fori_loop(0, tc, issue, 0)
    lax.fori_loop(0, tc, drain, 0)
    y = _layer_norm(alpha * x_ref[...] + (buf0[...] + buf1[...]), g_ref[...], b_ref[...])
    of_ref[...] = y
    ob_ref[...] = y.astype(BF16)


def _moe_combine(ys, slot0, slot1, x_f32, ln_g, ln_b, alpha, *, tc):
    n, d = x_f32.shape
    return pl.pallas_call(
        functools.partial(_combine_kernel, tc=tc, alpha=alpha),
        grid=(n // tc,),
        in_specs=[pl.BlockSpec((tc,), lambda t: (t,), memory_space=pltpu.SMEM),
                  pl.BlockSpec((tc,), lambda t: (t,), memory_space=pltpu.SMEM),
                  pl.BlockSpec(memory_space=pl.ANY),
                  pl.BlockSpec((tc, d), lambda t: (t, 0)),
                  pl.BlockSpec((1, d), lambda t: (0, 0)),
                  pl.BlockSpec((1, d), lambda t: (0, 0))],
        out_specs=[pl.BlockSpec((tc, d), lambda t: (t, 0)), pl.BlockSpec((tc, d), lambda t: (t, 0))],
        out_shape=[jax.ShapeDtypeStruct((n, d), F32), jax.ShapeDtypeStruct((n, d), BF16)],
        scratch_shapes=[pltpu.VMEM((tc, d), F32), pltpu.VMEM((tc, d), F32), pltpu.SemaphoreType.DMA(())],
        compiler_params=_cparams(("arbitrary",)),
        name="moe_combine",
    )(slot0, slot1, ys, x_f32, ln_g.reshape(1, d), ln_b.reshape(1, d))


def _moe_plan(route, n_experts, tm):
    n = route.shape[0]
    e = jnp.concatenate([route[:, 0], route[:, 1]]).astype(jnp.int32)
    gate = jnp.concatenate([route[:, 2], route[:, 3]])
    token = jnp.concatenate([jnp.arange(n, dtype=jnp.int32)] * TOP_K)
    onehot = (e[:, None] == jnp.arange(n_experts, dtype=jnp.int32)[None, :]).astype(jnp.int32)
    csum = jnp.cumsum(onehot, axis=0)
    rank = jnp.take_along_axis(csum, e[:, None], axis=1)[:, 0] - 1
    counts = csum[-1]
    tiles_per = (counts + tm - 1) // tm
    tile_end = jnp.cumsum(tiles_per)
    row_start = (tile_end - tiles_per) * tm
    dest = row_start[e] + rank
    n_tiles = (TOP_K * n + n_experts * (tm - 1)) // tm
    rows = n_tiles * tm
    src = jnp.zeros((rows,), jnp.int32).at[dest].set(token)
    gate_rows = jnp.zeros((rows,), F32).at[dest].set(gate)
    tile_expert = jnp.minimum(
        jnp.searchsorted(tile_end, jnp.arange(n_tiles, dtype=jnp.int32), side="right"), n_experts - 1).astype(jnp.int32)
    local_tile = jnp.arange(n_tiles, dtype=jnp.int32) - (tile_end - tiles_per)[tile_expert]
    tile_rows = jnp.clip(counts[tile_expert] - local_tile * tm, 0, tm).astype(jnp.int32)
    return src, gate_rows.reshape(rows, 1), tile_expert, tile_rows, dest[:n], dest[n:]


def _cast_rows(total_rows, steps):
    r = 16
    while total_rows % r or total_rows // r > steps:
        r += 16
    return r


def _ffn_dense(x_f32, x_bf, w1, w3, w2, ln_g, ln_b, alpha, cfg, to_cast=()):
    n, d = x_bf.shape
    f = w1.shape[-1]
    tm, tf = cfg["ffn_tm"], cfg["ffn_tf"]
    te = jnp.zeros((n // tm,), jnp.int32)
    nu = jnp.full((n // tm,), tm, jnp.int32)
    steps = (f // tf) * (n // tm)
    casts = []
    for c, arr in enumerate(to_cast):
        flat = arr.reshape(-1, arr.shape[-1])
        rows = _cast_rows(flat.shape[0], steps // len(to_cast) + steps // 64)
        start = c * (steps - flat.shape[0] // rows) // max(len(to_cast) - 1, 1)
        casts.append((flat, rows, start))
    a, *copies = _matmul_wres(x_bf, [w1[None], w3[None]], te, nu, tm=tm, tn=tf, col0=0, n_cols=f,
                              out_dtypes=[BF16], epilogue=_epi_swiglu, name="ffn_gate_up", casts=casts)
    y, y_bf = _proj_resid_ln(a, w2, x_f32, ln_g, ln_b, alpha, tm=cfg["ln_tm"], tk=cfg["down_tk"], name="ffn_down_ln")
    return y, y_bf, [cp.reshape(arr.shape) for cp, arr in zip(copies, to_cast)]


def _ffn_moe(x_f32, x_bf, w_router_pad, n_experts, w1, w3, w2_f32, ln_g, ln_b, alpha, cfg):
    n, d = x_bf.shape
    f = w1.shape[-1]
    tm, tf = cfg["moe_tm"], cfg["ffn_tf"]
    route = _router(x_f32, w_router_pad, n_experts, tm=cfg["row_tm"])
    src, gate_rows, tile_expert, tile_rows, slot0, slot1 = _moe_plan(route, n_experts, tm)
    xs = _gather_rows(x_f32, src, tr=cfg["gather_tr"])
    steps = (f // tf) * (xs.shape[0] // tm)
    a, w2 = _matmul_wres(xs, [w1, w3], tile_expert, tile_rows, tm=tm, tn=tf, col0=0, n_cols=f,
                         out_dtypes=[BF16], epilogue=_epi_swiglu, name="moe_gate_up",
                         casts=[(w2_f32.reshape(n_experts * f, d), _cast_rows(n_experts * f, steps), 0)])
    w2 = w2.reshape(n_experts, f, d)
    ys, = _matmul_acc(a, w2, tile_expert, tile_rows, [gate_rows],
                      [pl.BlockSpec((tm, 1), lambda t, k, te, nu: (t, 0))],
                      tm=tm, tk=cfg["moe_down_tk"], out_dtypes=[F32], epilogue=_epi_row_scale, name="moe_down")
    return _moe_combine(ys, slot0, slot1, x_f32, ln_g, ln_b, alpha, tc=cfg["combine_tc"])


def _split_w_in(w_in):
    sizes = (W_A, W_A, W_A, H_A, W_B, W_B, W_B, QK_C, QK_C, W_C, W_C, GATE_RANK)
    offs = np.concatenate([[0], np.cumsum(sizes)])
    w_t = w_in.T
    part = lambda i: w_t[offs[i]:offs[i + 1]]
    qa, ka, va, fa, qb, kb, vb, qc, kc, vc, rc, ac = [part(i) for i in range(12)]
    main = jnp.concatenate([ka, va, kb, vb, qc, kc, vc, rc, qa, qb], axis=0).astype(BF16)
    d = w_in.shape[0]
    fa_t = jnp.concatenate([fa, jnp.zeros((16 - H_A, d), F32)], axis=0).astype(BF16)
    ac_t = jnp.concatenate([ac, jnp.zeros((LANE - GATE_RANK, d), F32)], axis=0).astype(BF16)
    return main, fa_t, ac_t


def _mixer(x_bf, dims, layer, caches, params, cfg):
    batch, seq, dec_batch, t_new = dims
    fk_all, fv_all, bk_all, bv_all, gla_all, c_flogf, c_bk, c_bv = caches
    w_main, w_fa_t, w_ac_pad, b_forget, rel_bias, w_gate_up, b_gate, gla_norm_g = params
    n = x_bf.shape[0]
    n_p = batch * seq
    past = fk_all.shape[1]
    n_band = bk_all.shape[1]
    assert seq % CHUNK == 0 and t_new == CHUNK and past % CHUNK == 0 and n_band % LANE == 0
    assert n_band == min(BAND_PAST, past) and n_p % t_new == 0

    tm, tn = cfg["proj_tm"], cfg["proj_tn"]
    te = jnp.zeros((n // tm,), jnp.int32)
    nu = jnp.full((n // tm,), tm, jnp.int32)
    zs_f32, zs = _matmul_wres(x_bf, [w_main[None]], te, nu, tm=tm, tn=tn, col0=0, n_cols=W_STATE,
                              out_dtypes=[F32, BF16], epilogue=_epi_store, name="proj_state", w_transposed=True)
    zr, = _matmul_wres(x_bf, [w_main[None]], te, nu, tm=tm, tn=tn, col0=W_STATE, n_cols=W_REST,
                       out_dtypes=[BF16], epilogue=_epi_store, name="proj_rest", w_transposed=True)

    b_col = jnp.zeros((16, 1), F32).at[:H_A, 0].set(b_forget)
    logf_t = _forget_logits(x_bf, w_fa_t, b_col, tm=cfg["row_tm"])
    w_up_pad = jnp.zeros((LANE, QK_C), F32).at[:GATE_RANK].set(w_gate_up).astype(BF16)
    g = _gla_gates(x_bf, w_ac_pad, w_up_pad, b_gate, tm=cfg["row_tm"])

    hp = cfg["fox_hp"]
    logf_bh = logf_t[:H_A, :n_p].reshape(H_A, batch, seq).transpose(1, 0, 2).reshape(batch * H_A, seq)
    c_p = _cumsum_last(logf_bh).reshape(batch * H_A // hp, hp, seq)
    o_cat = jnp.zeros((n, MIX_WIDTH), BF16)
    o_cat = _fox_prompt(o_cat, zr, zs, c_p, batch, seq, tq=cfg["fox_tq"], hp=hp)
    logf_s = logf_t[:H_A, n_p:].reshape(H_A, dec_batch, t_new).transpose(1, 0, 2)
    logf_all = jnp.concatenate([c_flogf.transpose(0, 2, 1), logf_s], axis=2)
    c_s = _cumsum_last(logf_all.reshape(dec_batch * H_A, past + t_new))
    hg = cfg["fox_hg"]
    c_s = c_s.reshape(dec_batch * (H_A // hg), hg, past + t_new)
    cache_b0 = layer * dec_batch
    o_cat = _fox_sample(o_cat, zr, zs, fk_all, fv_all, cache_b0, c_s, n_p // t_new, dec_batch, t_new, hg=hg)

    bias_p, bias_s = _band_bias(rel_bias, tq=cfg["band_tq"], t_new=t_new, n_past=n_band)
    o_cat = _band_prompt(o_cat, zr, zs, bias_p, batch, seq, tq=cfg["band_tq"])
    o_cat = _band_sample(o_cat, zr, zs, bk_all, bv_all, cache_b0, bias_s, n_p // t_new, dec_batch, t_new)

    s_zero = jnp.zeros((batch, H_C, DK_C, DV_C), F32)
    o_cat, s_p = _gla(o_cat, zr, g, s_zero, 0, gla_norm_g, 0, batch, seq // CHUNK, chunk=CHUNK)
    o_cat, s_s = _gla(o_cat, zr, g, gla_all, cache_b0, gla_norm_g, n_p // CHUNK, dec_batch, 1, chunk=CHUNK)

    ka, va = zs_f32[:, ST_KA:ST_KA + W_A], zs_f32[:, ST_VA:ST_VA + W_A]
    kb, vb = zs_f32[:, ST_KB:ST_KB + W_B], zs_f32[:, ST_VB:ST_VB + W_B]
    heads = lambda t, b, l, h: t.reshape(b, l, h, HEAD_DIM)
    nbp = min(BAND_PAST, seq)
    logf_p = logf_t[:H_A, :n_p].reshape(H_A, batch, seq).transpose(1, 2, 0)
    st_p = (heads(ka[:n_p], batch, seq, H_A), heads(va[:n_p], batch, seq, H_A), logf_p,
            heads(kb[:n_p], batch, seq, H_B)[:, seq - nbp:], heads(vb[:n_p], batch, seq, H_B)[:, seq - nbp:], s_p)
    kb_s, vb_s = heads(kb[n_p:], dec_batch, t_new, H_B), heads(vb[n_p:], dec_batch, t_new, H_B)
    st_s = (heads(ka[n_p:], dec_batch, t_new, H_A), heads(va[n_p:], dec_batch, t_new, H_A),
            logf_s.transpose(0, 2, 1),
            jnp.concatenate([c_bk, kb_s], axis=1)[:, t_new:], jnp.concatenate([c_bv, vb_s], axis=1)[:, t_new:], s_s)
    return o_cat, st_p, st_s


def _config(n, d, f, seq):
    return dict(
        proj_tm=_tile(n, 1024, 256), proj_tn=_tile(W_STATE, 1024, 512),
        row_tm=_tile(n, 512, 256),
        ffn_tm=_tile(n, 512, 256), ffn_tf=_tile(f, 512, 128),
        ln_tm=_tile(n, 512, 128), down_tk=_tile(f, 512, 128), out_tk=_tile(MIX_WIDTH, 512, 128),
        moe_down_tk=_tile(f, 1024, 128),
        moe_tm=512 if n >= 4096 else 128, gather_tr=256 if n >= 4096 else 128,
        combine_tc=_tile(n, 256, 128),
        fox_tq=_tile(seq, 512, 128), fox_hp=4, fox_hg=4, band_tq=_tile(seq, 512, 128),
    )


def kernel(x_prompt, x_sample, cache_fox_k, cache_fox_v, cache_fox_logf, cache_band_k, cache_band_v, state_gla,
           w_in, b_forget, rel_bias, w_gate_up, b_gate, gla_norm_g, w_out,
           ln1_g, ln1_b, ln2_g, ln2_b, ffn_w1, ffn_w3, ffn_w2,
           moe_router, moe_w1, moe_w3, moe_w2):
    batch, seq, d = x_prompt.shape
    dec_batch, t_new, _ = x_sample.shape
    depth = w_in.shape[0]
    f = ffn_w1.shape[-1]
    n_experts = moe_router.shape[-1]
    n_p, n_s = batch * seq, dec_batch * t_new
    n = n_p + n_s
    alpha = (2.0 * depth) ** 0.25
    cfg = _config(n, d, f, seq)
    dims = (batch, seq, dec_batch, t_new)

    x = jnp.concatenate([x_prompt.reshape(n_p, d), x_sample.reshape(n_s, d)], axis=0)
    x_bf = x.astype(BF16)
    sp_all, ss_all = [], []
    past, n_band = cache_fox_k.shape[2], cache_band_k.shape[2]
    head_major = lambda c: c.transpose(0, 1, 3, 2, 4).reshape(depth * dec_batch * H_A, past, HEAD_DIM)
    fk_all, fv_all = head_major(cache_fox_k), head_major(cache_fox_v)
    bk_all = cache_band_k.reshape(depth * dec_batch, n_band, W_B)
    bv_all = cache_band_v.reshape(depth * dec_batch, n_band, W_B)
    gla_all = state_gla.reshape(depth * dec_batch, H_C, DK_C, DV_C)
    for i in range(depth):
        w_main, w_fa_t, w_ac_pad = _split_w_in(w_in[i])
        caches = (fk_all, fv_all, bk_all, bv_all, gla_all, cache_fox_logf[i], cache_band_k[i], cache_band_v[i])
        params = (w_main, w_fa_t, w_ac_pad, b_forget[i], rel_bias[i], w_gate_up[i], b_gate[i], gla_norm_g[i])
        o_cat, st_p, st_s = _mixer(x_bf, dims, i, caches, params, cfg)
        sp_all.append(st_p)
        ss_all.append(st_s)
        w_o = w_out[i]
        w_o = jnp.concatenate([w_o[W_A + W_B:], w_o[:W_A], w_o[W_A:W_A + W_B]], axis=0).astype(BF16)
        x, x_bf = _proj_resid_ln(o_cat, w_o, x, ln1_g[i], ln1_b[i], alpha,
                                 tm=cfg["ln_tm"], tk=cfg["out_tk"], name="out_proj_ln")
        j = i // 2
        if i % 2 == 0:
            to_cast = (moe_w1[j], moe_w3[j]) if i + 1 < depth else ()
            x, x_bf, moe_gate_up_bf = _ffn_dense(x, x_bf, ffn_w1[j].astype(BF16), ffn_w3[j].astype(BF16),
                                                 ffn_w2[j].astype(BF16), ln2_g[i], ln2_b[i], alpha, cfg, to_cast)
        else:
            w_router_pad = jnp.zeros((d, LANE), F32).at[:, :n_experts].set(moe_router[j])
            x, x_bf = _ffn_moe(x, x_bf, w_router_pad, n_experts, *moe_gate_up_bf,
                               moe_w2[j], ln2_g[i], ln2_b[i], alpha, cfg)
    stack = lambda states, k: jnp.stack([s[k] for s in states], axis=0)
    y_p = x[:n_p].reshape(batch, seq, d)
    y_s = x[n_p:].reshape(dec_batch, t_new, d)
    return (y_p, y_s) + tuple(stack(sp_all, k) for k in range(6)) + tuple(stack(ss_all, k) for k in range(6))
```
